```python
import math, functools
import jax, jax.numpy as jnp
from jax import lax
import numpy as np

D_MODEL = 1024
BATCH = 8
SEQ = 8192
DEPTH = 1
DEC_BATCH = 32
DEC_SEQ = 16
PAST_LEN = 4096

CHUNK = 64
EPS = 1e-6
SSD_HEADS = 16
SSD_HEAD_DIM = 64
D_SSM = SSD_HEADS * SSD_HEAD_DIM
SSD_GROUPS = 2
D_STATE = 128
CONV_W = 4
D_CONV = D_SSM + 2 * SSD_GROUPS * D_STATE
FOX_HEADS = 8
FOX_HEAD_DIM = 64
D_FOX = FOX_HEADS * FOX_HEAD_DIM
Q_BLOCK = 128
D_MIX = D_SSM + D_FOX
SPLIT_SIZES = (D_SSM, D_CONV, SSD_HEADS, D_FOX, D_FOX, D_FOX, FOX_HEADS)
D_IN_PROJ = sum(SPLIT_SIZES)
D_FF = 4 * D_MODEL

kernel_name = "hybrid_ssd_fox_streaming_step"


def rms_norm(x, g):
    xf = x.astype(jnp.float32)
    y = xf * lax.rsqrt(jnp.mean(xf * xf, axis=-1, keepdims=True) + EPS)
    return (y * g.astype(jnp.float32)).astype(x.dtype)


def split_proj(u):
    idx = [int(i) for i in np.cumsum(SPLIT_SIZES)[:-1]]
    return jnp.split(u, idx, axis=-1)


def causal_conv(xbc, conv_prev, conv_w, conv_b):
    L = xbc.shape[1]
    xp = jnp.concatenate([conv_prev.astype(xbc.dtype), xbc], axis=1)
    out = conv_b + sum(xp[:, k:k + L] * conv_w[k] for k in range(CONV_W))
    return jax.nn.silu(out), xp[:, -(CONV_W - 1):]


def ssd_scan(x, dt, A, Bm, Cm, state0):
    b, L, H, P = x.shape
    cl = min(CHUNK, L)
    nc = L // cl
    R = H // SSD_GROUPS

    def chunks(t):
        return jnp.moveaxis(t.reshape((b, nc, cl) + t.shape[2:]), 1, 0)

    xs = chunks(x.reshape(b, L, SSD_GROUPS, R, P) * dt.reshape(b, L, SSD_GROUPS, R)[..., None])
    dA = chunks((dt * A).reshape(b, L, SSD_GROUPS, R))
    Bs, Cs = chunks(Bm), chunks(Cm)
    mask = jnp.tril(jnp.ones((cl, cl), dtype=bool))[None, :, :, None, None]

    def step(state, inp):
        xc, dAc, Bc, Cc = inp
        acs = jnp.cumsum(dAc, axis=1)
        seg = acs[:, :, None] - acs[:, None, :]
        decay = jnp.exp(jnp.where(mask, seg, -jnp.inf))
        cb = jnp.einsum("blgn,bsgn->blsg", Cc, Bc)
        y_diag = jnp.einsum("blsg,blsgr,bsgrp->blgrp", cb, decay, xc)
        y_off = jnp.einsum("blgn,bgrpn,blgr->blgrp", Cc, state, jnp.exp(acs))
        tail = jnp.exp(acs[:, -1:] - acs)
        new_state = state * jnp.exp(acs[:, -1])[..., None, None] + jnp.einsum(
            "bsgn,bsgr,bsgrp->bgrpn", Bc, tail, xc)
        return new_state, y_diag + y_off

    final, ys = lax.scan(step, state0.reshape(b, SSD_GROUPS, R, P, D_STATE), (xs, dA, Bs, Cs))
    y = jnp.moveaxis(ys, 0, 1).reshape(b, L, H, P)
    return y, final.reshape(b, H, P, D_STATE)


def ssd_mixer(z, xbc, dt_raw, conv_prev, ssm_prev, conv_w, conv_b, dt_bias, A_log, D_skip, ssd_norm_w):
    b, L, _ = z.shape
    f32 = jnp.float32
    xbc_c, conv_state = causal_conv(xbc, conv_prev, conv_w, conv_b)
    xs, Bm, Cm = jnp.split(xbc_c.astype(f32), [D_SSM, D_SSM + SSD_GROUPS * D_STATE], axis=-1)
    dt = jax.nn.softplus(dt_raw.astype(f32) + dt_bias.astype(f32))
    A = -jnp.exp(A_log.astype(f32))
    x_h = xs.reshape(b, L, SSD_HEADS, SSD_HEAD_DIM)
    y, ssm_state = ssd_scan(x_h, dt, A,
                            Bm.reshape(b, L, SSD_GROUPS, D_STATE),
                            Cm.reshape(b, L, SSD_GROUPS, D_STATE),
                            ssm_prev.astype(f32))
    y = y + x_h * D_skip.astype(f32)[:, None]
    y = y.reshape(b, L, D_SSM) * jax.nn.silu(z.astype(f32))
    yg = y.reshape(b, L, SSD_GROUPS, D_SSM // SSD_GROUPS)
    yg = yg * lax.rsqrt(jnp.mean(yg * yg, axis=-1, keepdims=True) + EPS)
    y = yg.reshape(b, L, D_SSM) * ssd_norm_w.astype(f32)
    return y.astype(z.dtype), ssm_state, conv_state


def fox_project(q, k, v, f_raw, f_bias, q_norm_w, k_norm_w):
    b, L, _ = q.shape
    shp = (b, L, FOX_HEADS, FOX_HEAD_DIM)
    q = rms_norm(q.reshape(shp), q_norm_w)
    k = rms_norm(k.reshape(shp), k_norm_w)
    v = v.reshape(shp)
    logf = jax.nn.log_sigmoid(f_raw.astype(jnp.float32) + f_bias.astype(jnp.float32))
    return q, k, v, logf


def fox_attend(q, k, v, cq, ck, q_pos, k_pos):
    f32 = jnp.float32
    s = jnp.einsum("bthd,bshd->bhts", q.astype(f32), k.astype(f32)) * (FOX_HEAD_DIM ** -0.5)
    bias = jnp.transpose(cq, (0, 2, 1))[..., :, None] - jnp.transpose(ck, (0, 2, 1))[..., None, :]
    allowed = k_pos[None, :] <= q_pos[:, None]
    s = jnp.where(allowed, s + bias, -jnp.inf)
    p = jax.nn.softmax(s, axis=-1)
    return jnp.einsum("bhts,bshd->bthd", p, v.astype(f32))


def fox_prompt(q, k, v, logf):
    b, L, H, d = q.shape
    c = jnp.cumsum(logf, axis=1)
    nb = L // Q_BLOCK
    qb = jnp.moveaxis(q.reshape(b, nb, Q_BLOCK, H, d), 1, 0)
    cb = jnp.moveaxis(c.reshape(b, nb, Q_BLOCK, H), 1, 0)
    pos = jnp.arange(L)
    pb = pos.reshape(nb, Q_BLOCK)
    out = lax.map(lambda a: fox_attend(a[0], k, v, a[1], c, a[2], pos), (qb, cb, pb))
    return jnp.moveaxis(out, 0, 1).reshape(b, L, H * d)


def fox_sample(q, k, v, logf, cache_k, cache_v, cache_logf):
    b, T, H, d = q.shape
    past = cache_k.shape[1]
    k_all = jnp.concatenate([cache_k.astype(jnp.float32), k.astype(jnp.float32)], axis=1)
    v_all = jnp.concatenate([cache_v.astype(jnp.float32), v.astype(jnp.float32)], axis=1)
    c_all = jnp.cumsum(jnp.concatenate([cache_logf.astype(jnp.float32), logf], axis=1), axis=1)
    q_pos = past + jnp.arange(T)
    k_pos = jnp.arange(past + T)
    out = fox_attend(q, k_all, v_all, c_all[:, past:], c_all, q_pos, k_pos)
    return out.reshape(b, T, H * d)


def trunk_layer(x, conv_prev, ssm_prev, fox_fn, norm1_w, w_in, conv_w, conv_b, dt_bias, A_log,
                D_skip, ssd_norm_w, f_bias, q_norm_w, k_norm_w, w_out, norm2_w, w_up, w_down):
    h = rms_norm(x, norm1_w)
    u = jnp.einsum("bld,de->ble", h, w_in)
    z, xbc, dt_raw, q, k, v, f_raw = split_proj(u)
    y_ssd, ssm_state, conv_state = ssd_mixer(z, xbc, dt_raw, conv_prev, ssm_prev, conv_w, conv_b,
                                             dt_bias, A_log, D_skip, ssd_norm_w)
    q, k, v, logf = fox_project(q, k, v, f_raw, f_bias, q_norm_w, k_norm_w)
    y_fox = fox_fn(q, k, v, logf).astype(x.dtype)
    mix = jnp.concatenate([y_ssd, y_fox], axis=-1)
    x = x + jnp.einsum("ble,ed->bld", mix, w_out)
    h = rms_norm(x, norm2_w)
    x = x + jnp.einsum("blf,fd->bld", jnp.square(jax.nn.relu(jnp.einsum("bld,df->blf", h, w_up))), w_down)
    return x, (k, v, logf, ssm_state, conv_state)


def setup_inputs(seed: int = 0) -> dict:
    key = jax.random.key(seed)
    ks = jax.random.split(key, 24)
    f32 = jnp.float32

    def nrm(k, shape, scale=1.0):
        return scale * jax.random.normal(k, shape, f32)

    x_prompt = nrm(ks[0], (BATCH, SEQ, D_MODEL))
    x_sample = nrm(ks[1], (DEC_BATCH, DEC_SEQ, D_MODEL))
    cache_k = nrm(ks[2], (DEPTH, DEC_BATCH, PAST_LEN, FOX_HEADS, FOX_HEAD_DIM))
    cache_v = nrm(ks[3], (DEPTH, DEC_BATCH, PAST_LEN, FOX_HEADS, FOX_HEAD_DIM))
    cache_logf = jax.nn.log_sigmoid(2.5 + nrm(ks[4], (DEPTH, DEC_BATCH, PAST_LEN, FOX_HEADS)))
    state_ssm = nrm(ks[5], (DEPTH, DEC_BATCH, SSD_HEADS, SSD_HEAD_DIM, D_STATE), 0.1)
    state_conv = nrm(ks[6], (DEPTH, DEC_BATCH, CONV_W - 1, D_CONV))
    norm1_w = 1.0 + nrm(ks[7], (DEPTH, D_MODEL), 0.01)
    w_in = nrm(ks[8], (DEPTH, D_MODEL, D_IN_PROJ), D_MODEL ** -0.5)
    conv_w = nrm(ks[9], (DEPTH, CONV_W, D_CONV), CONV_W ** -0.5)
    conv_b = nrm(ks[10], (DEPTH, D_CONV), 0.01)
    dt_init = jnp.exp(jax.random.uniform(ks[11], (DEPTH, SSD_HEADS), f32, math.log(1e-3), math.log(1e-1)))
    dt_bias = dt_init + jnp.log(-jnp.expm1(-dt_init))
    A_log = jnp.log(jax.random.uniform(ks[12], (DEPTH, SSD_HEADS), f32, 1.0, 16.0))
    D_skip = 1.0 + nrm(ks[13], (DEPTH, SSD_HEADS), 0.01)
    ssd_norm_w = 1.0 + nrm(ks[14], (DEPTH, D_SSM), 0.01)
    f_bias = jax.random.uniform(ks[15], (DEPTH, FOX_HEADS), f32, 1.0, 4.0)
    q_norm_w = 1.0 + nrm(ks[16], (DEPTH, FOX_HEAD_DIM), 0.01)
    k_norm_w = 1.0 + nrm(ks[17], (DEPTH, FOX_HEAD_DIM), 0.01)
    w_out = nrm(ks[18], (DEPTH, D_MIX, D_MODEL), D_MIX ** -0.5)
    norm2_w = 1.0 + nrm(ks[19], (DEPTH, D_MODEL), 0.01)
    w_up = nrm(ks[20], (DEPTH, D_MODEL, D_FF), D_MODEL ** -0.5)
    w_down = nrm(ks[21], (DEPTH, D_FF, D_MODEL), D_FF ** -0.5)
    return {"x_prompt": x_prompt, "x_sample": x_sample, "cache_k": cache_k, "cache_v": cache_v,
            "cache_logf": cache_logf, "state_ssm": state_ssm, "state_conv": state_conv,
            "norm1_w": norm1_w, "w_in": w_in, "conv_w": conv_w, "conv_b": conv_b, "dt_bias": dt_bias,
            "A_log": A_log, "D_skip": D_skip, "ssd_norm_w": ssd_norm_w, "f_bias": f_bias,
            "q_norm_w": q_norm_w, "k_norm_w": k_norm_w, "w_out": w_out, "norm2_w": norm2_w,
            "w_up": w_up, "w_down": w_down}


def reference(x_prompt, x_sample, cache_k, cache_v, cache_logf, state_ssm, state_conv,
              norm1_w, w_in, conv_w, conv_b, dt_bias, A_log, D_skip, ssd_norm_w, f_bias,
              q_norm_w, k_norm_w, w_out, norm2_w, w_up, w_down):
    b_p = x_prompt.shape[0]
    y_prompt, y_sample = x_prompt, x_sample
    p_states, s_states = [], []
    for l in range(DEPTH):
        wl = (norm1_w[l], w_in[l], conv_w[l], conv_b[l], dt_bias[l], A_log[l], D_skip[l],
              ssd_norm_w[l], f_bias[l], q_norm_w[l], k_norm_w[l], w_out[l], norm2_w[l], w_up[l], w_down[l])
        conv0 = jnp.zeros((b_p, CONV_W - 1, D_CONV), x_prompt.dtype)
        ssm0 = jnp.zeros((b_p, SSD_HEADS, SSD_HEAD_DIM, D_STATE), jnp.float32)
        y_prompt, st_p = trunk_layer(y_prompt, conv0, ssm0, fox_prompt, *wl)
        p_states.append(st_p)
        fox_fn = functools.partial(fox_sample, cache_k=cache_k[l], cache_v=cache_v[l], cache_logf=cache_logf[l])
        y_sample, st_s = trunk_layer(y_sample, state_conv[l], state_ssm[l], fox_fn, *wl)
        s_states.append(st_s)
    p_k = jnp.stack([s[0] for s in p_states])
    p_v = jnp.stack([s[1] for s in p_states])
    p_logf = jnp.stack([s[2] for s in p_states])
    p_ssm = jnp.stack([s[3] for s in p_states])
    p_conv = jnp.stack([s[4] for s in p_states])
    s_k = jnp.stack([s[0] for s in s_states])
    s_v = jnp.stack([s[1] for s in s_states])
    s_logf = jnp.stack([s[2] for s in s_states])
    s_ssm = jnp.stack([s[3] for s in s_states])
    s_conv = jnp.stack([s[4] for s in s_states])
    return (y_prompt, y_sample, p_k, p_v, p_logf, p_ssm, p_conv, s_k, s_v, s_logf, s_ssm, s_conv)
```

```python
import functools
import math

import jax
import jax.numpy as jnp
from jax import lax
from jax.experimental import pallas as pl
from jax.experimental.pallas import tpu as pltpu

F32 = jnp.float32
BF16 = jnp.bfloat16

EPS = 1e-6
SSD_HEADS = 16
SSD_HEAD_DIM = 64
SSD_GROUPS = 2
HEADS_PER_GROUP = SSD_HEADS // SSD_GROUPS
D_STATE = 128
CONV_W = 4
D_SSM = SSD_HEADS * SSD_HEAD_DIM
D_BC = SSD_GROUPS * D_STATE
D_CONV = D_SSM + 2 * D_BC
GROUP_W = D_SSM // SSD_GROUPS
FOX_HEADS = 8
FOX_HEAD_DIM = 64
D_FOX = FOX_HEADS * FOX_HEAD_DIM

LANES = 128
SUBLANES = 8
SSD_CHUNK = 128
VMEM_LIMIT = 56 * 1024 * 1024

C_Z = 0
C_XBC = C_Z + D_SSM
C_DT = C_XBC + D_CONV
C_Q = C_DT + D_SSM
C_K = C_Q + D_FOX
C_V = C_K + D_FOX
C_SMALL = C_V + D_FOX
N_PROJ = C_SMALL + LANES
SMALL_DT = 0
SMALL_F = SSD_HEADS


def _softplus(x):
    return jnp.maximum(x, 0.0) + jnp.log1p(jnp.exp(-jnp.abs(x)))


def _silu(x):
    return x * (1.0 / (1.0 + jnp.exp(-x)))


def _lane_cumsum(x):
    n = x.shape[-1]
    lane = lax.broadcasted_iota(jnp.int32, x.shape, x.ndim - 1)
    k = 1
    while k < n:
        x = x + jnp.where(lane >= k, pltpu.roll(x, k, axis=x.ndim - 1), 0.0)
        k *= 2
    return x


def _const_spec(shape):
    zeros = (0,) * len(shape)
    return pl.BlockSpec(shape, lambda *_: zeros, pipeline_mode=pl.Buffered(1))


def _in_proj_kernel(x_ref, g_ref, w_ref, bias_ref, bd_ref, qw_ref, kw_ref,
                    z_ref, xbc_ref, dt_ref, q_ref, kf_ref, kb_ref, vf_ref, vb_ref, small_ref,
                    *rest, tiles_per_seq, with_cumsum):
    x = x_ref[...]
    ms = jnp.mean(x * x, axis=-1, keepdims=True)
    h = (x * lax.rsqrt(ms + EPS) * g_ref[...]).astype(BF16)

    def proj(lo, hi):
        return jnp.dot(h, w_ref[:, lo:hi], preferred_element_type=F32)

    z_ref[...] = proj(C_Z, C_XBC).astype(BF16)
    xbc_ref[...] = proj(C_XBC, C_DT).astype(BF16)
    dt_ref[...] = _softplus(proj(C_DT, C_Q) + bias_ref[:, 0:D_SSM])

    def head_rms(t, w):
        ss = jnp.dot((t * t).astype(BF16), bd_ref[...], preferred_element_type=F32)
        return t * lax.rsqrt(ss * (1.0 / FOX_HEAD_DIM) + EPS) * w

    qn = head_rms(proj(C_Q, C_K), qw_ref[...])
    q_ref[...] = (qn * (FOX_HEAD_DIM ** -0.5)).astype(BF16)
    kn = head_rms(proj(C_K, C_V), kw_ref[...])
    kf_ref[...] = kn
    kb_ref[...] = kn.astype(BF16)
    v = proj(C_V, C_SMALL)
    vf_ref[...] = v
    vb_ref[...] = v.astype(BF16)

    sm = proj(C_SMALL, N_PROJ) + bias_ref[:, D_SSM:D_SSM + LANES]
    lane = lax.broadcasted_iota(jnp.int32, sm.shape, 1)
    sm = jnp.where(lane < SMALL_F, _softplus(sm), -_softplus(-sm))
    small_ref[...] = sm

    if with_cumsum:
        ct_ref, carry_ref = rest
        i = pl.program_id(0)

        @pl.when(i % tiles_per_seq == 0)
        def _():
            carry_ref[...] = jnp.zeros_like(carry_ref)

        lf_t = sm.T[SMALL_F:SMALL_F + FOX_HEADS, :]
        c = _lane_cumsum(lf_t) + carry_ref[:, 0:1]
        ct_ref[0] = c
        carry_ref[...] = jnp.broadcast_to(c[:, -1:], carry_ref.shape)


def _in_proj(x2d, g, w_all, bias_all, bd, qw, kw, *, tm, seq_len, with_cumsum):
    m, d = x2d.shape
    nt = m // tm
    row = lambda i: (i, 0)
    out_shape = [
        jax.ShapeDtypeStruct((m, D_SSM), BF16),
        jax.ShapeDtypeStruct((m, D_CONV), BF16),
        jax.ShapeDtypeStruct((m, D_SSM), F32),
        jax.ShapeDtypeStruct((m, D_FOX), BF16),
        jax.ShapeDtypeStruct((m, D_FOX), F32),
        jax.ShapeDtypeStruct((m, D_FOX), BF16),
        jax.ShapeDtypeStruct((m, D_FOX), F32),
        jax.ShapeDtypeStruct((m, D_FOX), BF16),
        jax.ShapeDtypeStruct((m, LANES), F32),
    ]
    out_specs = [
        pl.BlockSpec((tm, D_SSM), row), pl.BlockSpec((tm, D_CONV), row), pl.BlockSpec((tm, D_SSM), row),
        pl.BlockSpec((tm, D_FOX), row), pl.BlockSpec((tm, D_FOX), row), pl.BlockSpec((tm, D_FOX), row),
        pl.BlockSpec((tm, D_FOX), row), pl.BlockSpec((tm, D_FOX), row), pl.BlockSpec((tm, LANES), row),
    ]
    scratch = []
    if with_cumsum:
        out_shape.append(jax.ShapeDtypeStruct((nt, FOX_HEADS, tm), F32))
        out_specs.append(pl.BlockSpec((1, FOX_HEADS, tm), lambda i: (i, 0, 0)))
        scratch.append(pltpu.VMEM((FOX_HEADS, LANES), F32))
    kern = functools.partial(_in_proj_kernel, tiles_per_seq=max(seq_len // tm, 1), with_cumsum=with_cumsum)
    return pl.pallas_call(
        kern,
        grid=(nt,),
        in_specs=[
            pl.BlockSpec((tm, d), row),
            _const_spec((1, d)), _const_spec((d, N_PROJ)), _const_spec((1, D_SSM + LANES)),
            _const_spec((D_FOX, D_FOX)), _const_spec((1, D_FOX)), _const_spec((1, D_FOX)),
        ],
        out_specs=out_specs,
        out_shape=out_shape,
        scratch_shapes=scratch,
        compiler_params=pltpu.CompilerParams(dimension_semantics=("arbitrary",), vmem_limit_bytes=VMEM_LIMIT),
        name="in_proj",
    )(x2d, g, w_all, bias_all, bd, qw, kw)


def _pad_rows(x, rows):
    if x.shape[0] == rows:
        return x
    return jnp.concatenate([x, jnp.zeros((rows - x.shape[0],) + x.shape[1:], x.dtype)], axis=0)


def _ssd_kernel(xbc_ref, dt_ref, small_ref, z_ref, convprev_ref, st0_ref,
                convw_ref, convb_ref, alog_ref, dskip_ref, normw_ref, tri_ref,
                y_ref, stout_ref, convout_ref,
                xpad_scr, st_scr, ydiag_scr, *, lc):
    c = pl.program_id(1)
    P = SSD_CHUNK

    @pl.when(c == 0)
    def _():
        xpad_scr[0:SUBLANES] = convprev_ref[0]
        st_scr[...] = st0_ref[0]

    xpad_scr[SUBLANES:SUBLANES + lc] = xbc_ref[...].astype(F32)
    conv = convb_ref[...]
    for k in range(CONV_W):
        off = SUBLANES - (CONV_W - 1) + k
        conv = conv + convw_ref[k:k + 1, :] * xpad_scr[off:off + lc]
    last = xpad_scr[lc:lc + SUBLANES]
    xpad_scr[0:SUBLANES] = last
    convout_ref[0] = last

    act = _pad_rows(_silu(conv), P)
    dt = _pad_rows(dt_ref[...], P)
    dt_c = _pad_rows(small_ref[...], P)
    xs = act[:, :D_SSM]
    xdt = xs * dt

    a_all = -jnp.exp(alog_ref[...])
    da = jnp.concatenate([dt, dt_c], axis=1) * a_all
    da_hi = da.astype(BF16)
    da_lo = (da - da_hi.astype(F32)).astype(BF16)
    tri = tri_ref[...]
    acs_all = (jnp.dot(tri, da_hi, preferred_element_type=F32)
               + jnp.dot(tri, da_lo, preferred_element_type=F32))
    acs = acs_all[:, :D_SSM]
    acs_t = acs_all[:, D_SSM:].T
    a_last = acs[P - 1:P, :]
    ea = jnp.exp(acs)
    xt = (xdt * jnp.exp(a_last - acs)).astype(BF16)
    sd = jnp.exp(a_last)

    rowi = lax.broadcasted_iota(jnp.int32, (P, P), 0)
    coli = lax.broadcasted_iota(jnp.int32, (P, P), 1)
    tril = coli <= rowi
    lane = lax.broadcasted_iota(jnp.int32, (P, LANES), 1)
    first_half = lane < SSD_HEAD_DIM

    yoff = []
    for g in range(SSD_GROUPS):
        bg = act[:, D_SSM + g * D_STATE:D_SSM + (g + 1) * D_STATE]
        cg = act[:, D_SSM + D_BC + g * D_STATE:D_SSM + D_BC + (g + 1) * D_STATE]
        bgb = bg.astype(BF16)
        cgb = cg.astype(BF16)
        cb = lax.dot_general(cgb, bgb, (((1,), (1,)), ((), ())), preferred_element_type=F32)
        for pr in range(HEADS_PER_GROUP // 2):
            lo = g * GROUP_W + pr * LANES
            h0 = g * HEADS_PER_GROUP + 2 * pr
            slab = acs[:, lo:lo + LANES]
            rolled = pltpu.roll(slab, SSD_HEAD_DIM, axis=1)
            xpair = xdt[:, lo:lo + LANES]
            yhead = []
            for j in range(2):
                col = jnp.where(first_half, slab, rolled) if j == 0 else jnp.where(first_half, rolled, slab)
                seg = col - acs_t[h0 + j:h0 + j + 1, :]
                dec = jnp.exp(jnp.where(tril, seg, -jnp.inf))
                mh = (cb * dec).astype(BF16)
                keep = first_half if j == 0 else jnp.logical_not(first_half)
                xm = jnp.where(keep, xpair, 0.0).astype(BF16)
                yhead.append(jnp.dot(mh, xm, preferred_element_type=F32))
            ydiag_scr[:, lo:lo + LANES] = yhead[0] + yhead[1]
        gs = slice(g * GROUP_W, (g + 1) * GROUP_W)
        st = st_scr[g]
        yoff.append(jnp.dot(cgb, st.astype(BF16), preferred_element_type=F32) * ea[:, gs])
        st_scr[g] = st * sd[:, gs] + jnp.dot(bg.T.astype(BF16), xt[:, gs], preferred_element_type=F32)

    y = ydiag_scr[...] + jnp.concatenate(yoff, axis=1) + xs * dskip_ref[...]
    y = y[:lc] * _silu(z_ref[...].astype(F32))
    outs = []
    for g in range(SSD_GROUPS):
        yg = y[:, g * GROUP_W:(g + 1) * GROUP_W]
        outs.append(yg * lax.rsqrt(jnp.mean(yg * yg, axis=-1, keepdims=True) + EPS))
    y_ref[...] = (jnp.concatenate(outs, axis=1) * normw_ref[...]).astype(BF16)
    stout_ref[0] = st_scr[...]


def _ssd(xbc, dt, small, z, conv_prev8, st0, conv_w, conv_b, alog_all, dskip, normw, tri, *, batch, seq_len):
    lc = min(SSD_CHUNK, seq_len)
    nc = seq_len // lc
    blk = lambda b, c: (b * nc + c, 0)
    per_b3 = lambda b, c: (b, 0, 0)
    per_b4 = lambda b, c: (b, 0, 0, 0)
    m = batch * seq_len
    return pl.pallas_call(
        functools.partial(_ssd_kernel, lc=lc),
        grid=(batch, nc),
        in_specs=[
            pl.BlockSpec((lc, D_CONV), blk), pl.BlockSpec((lc, D_SSM), blk), pl.BlockSpec((lc, LANES), blk),
            pl.BlockSpec((lc, D_SSM), blk),
            pl.BlockSpec((1, SUBLANES, D_CONV), per_b3),
            pl.BlockSpec((1, SSD_GROUPS, D_STATE, GROUP_W), per_b4),
            _const_spec((CONV_W, D_CONV)), _const_spec((1, D_CONV)), _const_spec((1, D_SSM + LANES)),
            _const_spec((1, D_SSM)), _const_spec((1, D_SSM)), _const_spec((SSD_CHUNK, SSD_CHUNK)),
        ],
        out_specs=[
            pl.BlockSpec((lc, D_SSM), blk),
            pl.BlockSpec((1, SSD_GROUPS, D_STATE, GROUP_W), per_b4),
            pl.BlockSpec((1, SUBLANES, D_CONV), per_b3),
        ],
        out_shape=[
            jax.ShapeDtypeStruct((m, D_SSM), BF16),
            jax.ShapeDtypeStruct((batch, SSD_GROUPS, D_STATE, GROUP_W), F32),
            jax.ShapeDtypeStruct((batch, SUBLANES, D_CONV), F32),
        ],
        scratch_shapes=[
            pltpu.VMEM((lc + SUBLANES, D_CONV), F32),
            pltpu.VMEM((SSD_GROUPS, D_STATE, GROUP_W), F32),
            pltpu.VMEM((SSD_CHUNK, D_SSM), F32),
        ],
        compiler_params=pltpu.CompilerParams(dimension_semantics=("arbitrary", "arbitrary"),
                                             vmem_limit_bytes=VMEM_LIMIT),
        name="ssd",
    )(xbc, dt, small, z, conv_prev8, st0, conv_w, conv_b, alog_all, dskip, normw, tri)


def _fox_prompt_kernel(q_ref, k_ref, v_ref, ct_ref, o_ref, qm_scr, m_scr, l_scr, acc_scr, *, t):
    p = pl.program_id(1)
    qi = pl.program_id(2)
    lane = lax.broadcasted_iota(jnp.int32, (t, LANES), 1)
    first_half = lane < FOX_HEAD_DIM
    q = q_ref[...]
    qm_scr[0] = jnp.where(first_half, q, jnp.zeros_like(q))
    qm_scr[1] = jnp.where(first_half, jnp.zeros_like(q), q)
    m_scr[...] = jnp.full(m_scr.shape, -jnp.inf, F32)
    l_scr[...] = jnp.zeros(l_scr.shape, F32)
    acc_scr[...] = jnp.zeros(acc_scr.shape, F32)

    def tile(ki, masked):
        start = pl.multiple_of(ki * t, t)
        k = k_ref[pl.ds(start, t), :]
        v = v_ref[pl.ds(start, t), :]
        for j in range(2):
            ck = ct_ref[ki, pl.ds(2 * p + j, 1), :]
            s = lax.dot_general(qm_scr[j], k, (((1,), (1,)), ((), ())), preferred_element_type=F32) - ck
            if masked:
                rowi = lax.broadcasted_iota(jnp.int32, (t, t), 0)
                coli = lax.broadcasted_iota(jnp.int32, (t, t), 1)
                s = jnp.where(coli <= rowi, s, -jnp.inf)
            m_prev = m_scr[j]
            m_new = jnp.maximum(m_prev, jnp.max(s, axis=-1, keepdims=True))
            alpha = jnp.exp(m_prev - m_new)
            pexp = jnp.exp(s - m_new)
            l_scr[j] = alpha * l_scr[j] + jnp.sum(pexp, axis=-1, keepdims=True)
            acc_scr[j] = alpha * acc_scr[j] + jnp.dot(pexp.astype(BF16), v, preferred_element_type=F32)
            m_scr[j] = m_new

    def body(ki, carry):
        tile(ki, False)
        return carry

    lax.fori_loop(0, qi, body, 0)
    tile(qi, True)
    o = jnp.where(first_half, acc_scr[0] / l_scr[0], acc_scr[1] / l_scr[1])
    o_ref[...] = o.astype(BF16)


def _fox_prompt(q, k, v, ct, *, batch, seq_len, t):
    nq = seq_len // t
    m = batch * seq_len
    return pl.pallas_call(
        functools.partial(_fox_prompt_kernel, t=t),
        grid=(batch, FOX_HEADS // 2, nq),
        in_specs=[
            pl.BlockSpec((t, LANES), lambda b, p, i: (b * nq + i, p)),
            pl.BlockSpec((seq_len, LANES), lambda b, p, i: (b, p)),
            pl.BlockSpec((seq_len, LANES), lambda b, p, i: (b, p)),
            pl.BlockSpec((nq, FOX_HEADS, t), lambda b, p, i: (b, 0, 0)),
        ],
        out_specs=pl.BlockSpec((t, LANES), lambda b, p, i: (b * nq + i, p)),
        out_shape=jax.ShapeDtypeStruct((m, D_FOX), BF16),
        scratch_shapes=[
            pltpu.VMEM((2, t, LANES), BF16),
            pltpu.VMEM((2, t, 1), F32),
            pltpu.VMEM((2, t, 1), F32),
            pltpu.VMEM((2, t, LANES), F32),
        ],
        compiler_params=pltpu.CompilerParams(dimension_semantics=("arbitrary", "arbitrary", "arbitrary"),
                                             vmem_limit_bytes=VMEM_LIMIT),
        name="fox_prompt",
    )(q, k, v, ct)


def _fox_sample_kernel(q_ref, kc_ref, vc_ref, lfc_ref, kn_ref, vn_ref, lfn_ref, own_ref, qpos_ref, o_ref,
                       qbd_scr, m_scr, l_scr, acc_scr, carry_scr, *, tn):
    j = pl.program_id(1)
    nk = pl.num_programs(1)
    rows = FOX_HEADS * tn
    own = own_ref[...] > 0.0

    @pl.when(j == 0)
    def _():
        qt = jnp.concatenate([q_ref[...]] * FOX_HEADS, axis=0)
        qbd_scr[...] = jnp.where(own, qt, jnp.zeros_like(qt))
        m_scr[...] = jnp.full(m_scr.shape, -jnp.inf, F32)
        l_scr[...] = jnp.zeros(l_scr.shape, F32)
        acc_scr[...] = jnp.zeros(acc_scr.shape, F32)
        carry_scr[...] = jnp.zeros(carry_scr.shape, F32)

    def expand(ct):
        n = ct.shape[-1]
        return jnp.concatenate([jnp.broadcast_to(ct[h:h + 1], (tn, n)) for h in range(FOX_HEADS)], axis=0)

    def update(s, vals):
        m_prev = m_scr[...]
        m_new = jnp.maximum(m_prev, jnp.max(s, axis=-1, keepdims=True))
        alpha = jnp.exp(m_prev - m_new)
        pexp = jnp.exp(s - m_new)
        l_scr[...] = alpha * l_scr[...] + jnp.sum(pexp, axis=-1, keepdims=True)
        acc_scr[...] = alpha * acc_scr[...] + jnp.dot(pexp.astype(BF16), vals, preferred_element_type=F32)
        m_scr[...] = m_new

    nt_dims = (((1,), (1,)), ((), ()))
    ct = _lane_cumsum(lfc_ref[0]) + carry_scr[:, 0:1]
    carry_scr[...] = jnp.broadcast_to(ct[:, -1:], carry_scr.shape)
    kb = kc_ref[0].astype(BF16)
    s = lax.dot_general(qbd_scr[...], kb, nt_dims, preferred_element_type=F32) - expand(ct)
    update(s, vc_ref[0].astype(BF16))

    @pl.when(j == nk - 1)
    def _():
        cn = _lane_cumsum(lfn_ref[0]) + carry_scr[:, 0:1]
        kn = _pad_rows(kn_ref[...], LANES)
        vn = _pad_rows(vn_ref[...], LANES)
        sn = lax.dot_general(qbd_scr[...], kn, nt_dims, preferred_element_type=F32) - expand(cn)
        cidx = lax.broadcasted_iota(jnp.int32, (rows, LANES), 1)
        sn = jnp.where(cidx <= qpos_ref[...], sn, -jnp.inf)
        update(sn, vn)
        o = acc_scr[...] / l_scr[...]
        o = jnp.where(own, o, 0.0)
        out = o[0:tn]
        for h in range(1, FOX_HEADS):
            out = out + o[h * tn:(h + 1) * tn]
        o_ref[...] = out.astype(BF16)


def _fox_sample(q, kc, vc, lfc_t, kn, vn, lfn_t, *, batch, tn, past, tk):
    nk = past // tk
    rows = FOX_HEADS * tn
    r = jnp.arange(rows)
    own = ((r // tn)[:, None] == (jnp.arange(D_FOX) // FOX_HEAD_DIM)[None, :]).astype(F32)
    qpos = (r % tn).astype(jnp.int32).reshape(rows, 1)
    return pl.pallas_call(
        functools.partial(_fox_sample_kernel, tn=tn),
        grid=(batch, nk),
        in_specs=[
            pl.BlockSpec((tn, D_FOX), lambda b, j: (b, 0)),
            pl.BlockSpec((1, tk, D_FOX), lambda b, j: (b, j, 0)),
            pl.BlockSpec((1, tk, D_FOX), lambda b, j: (b, j, 0)),
            pl.BlockSpec((1, FOX_HEADS, tk), lambda b, j: (b, 0, j)),
            pl.BlockSpec((tn, D_FOX), lambda b, j: (b, 0)),
            pl.BlockSpec((tn, D_FOX), lambda b, j: (b, 0)),
            pl.BlockSpec((1, FOX_HEADS, LANES), lambda b, j: (b, 0, 0)),
            _const_spec((rows, D_FOX)), _const_spec((rows, 1)),
        ],
        out_specs=pl.BlockSpec((tn, D_FOX), lambda b, j: (b, 0)),
        out_shape=jax.ShapeDtypeStruct((batch * tn, D_FOX), BF16),
        scratch_shapes=[
            pltpu.VMEM((rows, D_FOX), BF16),
            pltpu.VMEM((rows, 1), F32),
            pltpu.VMEM((rows, 1), F32),
            pltpu.VMEM((rows, D_FOX), F32),
            pltpu.VMEM((FOX_HEADS, LANES), F32),
        ],
        compiler_params=pltpu.CompilerParams(dimension_semantics=("arbitrary", "arbitrary"),
                                             vmem_limit_bytes=VMEM_LIMIT),
        name="fox_sample",
    )(q, kc, vc, lfc_t, kn, vn, lfn_t, own, qpos)


def _out_mlp_kernel(x_ref, ys_ref, yf_ref, wos_ref, wof_ref, g_ref, wup_ref, wdn_ref, o_ref):
    x1 = (x_ref[...]
          + jnp.dot(ys_ref[...], wos_ref[...], preferred_element_type=F32)
          + jnp.dot(yf_ref[...], wof_ref[...], preferred_element_type=F32))
    ms = jnp.mean(x1 * x1, axis=-1, keepdims=True)
    h = (x1 * lax.rsqrt(ms + EPS) * g_ref[...]).astype(BF16)
    up = jnp.dot(h, wup_ref[...], preferred_element_type=F32)
    a = jnp.square(jnp.maximum(up, 0.0)).astype(BF16)
    o_ref[...] = x1 + jnp.dot(a, wdn_ref[...], preferred_element_type=F32)


def _out_mlp(x2d, ys, yf, wo_s, wo_f, g2, w_up, w_dn, *, tm):
    m, d = x2d.shape
    d_ff = w_up.shape[1]
    row = lambda i: (i, 0)
    return pl.pallas_call(
        _out_mlp_kernel,
        grid=(m // tm,),
        in_specs=[
            pl.BlockSpec((tm, d), row), pl.BlockSpec((tm, D_SSM), row), pl.BlockSpec((tm, D_FOX), row),
            _const_spec((D_SSM, d)), _const_spec((D_FOX, d)), _const_spec((1, d)),
            _const_spec((d, d_ff)), _const_spec((d_ff, d)),
        ],
        out_specs=pl.BlockSpec((tm, d), row),
        out_shape=jax.ShapeDtypeStruct((m, d), F32),
        compiler_params=pltpu.CompilerParams(dimension_semantics=("arbitrary",), vmem_limit_bytes=VMEM_LIMIT),
        name="out_mlp",
    )(x2d, ys, yf, wo_s, wo_f, g2, w_up, w_dn)


def _state_to_kernel_layout(st):
    b = st.shape[0]
    st = st.reshape(b, SSD_GROUPS, HEADS_PER_GROUP, SSD_HEAD_DIM, D_STATE)
    return st.transpose(0, 1, 4, 2, 3).reshape(b, SSD_GROUPS, D_STATE, GROUP_W)


def _state_from_kernel_layout(st):
    b = st.shape[0]
    st = st.reshape(b, SSD_GROUPS, D_STATE, HEADS_PER_GROUP, SSD_HEAD_DIM)
    return st.transpose(0, 1, 3, 4, 2).reshape(b, SSD_HEADS, SSD_HEAD_DIM, D_STATE)


def _layer_weights(norm1_w, w_in, conv_w, conv_b, dt_bias, A_log, D_skip, ssd_norm_w, f_bias,
                   q_norm_w, k_norm_w, w_out, norm2_w, w_up, w_down):
    d = w_in.shape[0]
    splits = [D_SSM, D_CONV, SSD_HEADS, D_FOX, D_FOX, D_FOX]
    idx = [sum(splits[:i + 1]) for i in range(len(splits))]
    w_z, w_xbc, w_dt, w_q, w_k, w_v, w_f = jnp.split(w_in, idx, axis=1)
    pad = jnp.zeros((d, LANES - SSD_HEADS - FOX_HEADS), w_in.dtype)
    w_all = jnp.concatenate(
        [w_z, w_xbc, jnp.repeat(w_dt, SSD_HEAD_DIM, axis=1), w_q, w_k, w_v, w_dt, w_f, pad], axis=1).astype(BF16)
    bias_all = jnp.concatenate(
        [jnp.repeat(dt_bias, SSD_HEAD_DIM), dt_bias, f_bias,
         jnp.zeros((LANES - SSD_HEADS - FOX_HEADS,), F32)]).reshape(1, D_SSM + LANES).astype(F32)
    head_id = jnp.arange(D_FOX) // FOX_HEAD_DIM
    bd = (head_id[:, None] == head_id[None, :]).astype(BF16)
    alog_all = jnp.concatenate(
        [jnp.repeat(A_log, SSD_HEAD_DIM), A_log, jnp.zeros((LANES - SSD_HEADS,), F32)]).reshape(1, D_SSM + LANES)
    r = jnp.arange(SSD_CHUNK)
    tri = (r[None, :] <= r[:, None]).astype(BF16)
    return dict(
        g1=norm1_w.reshape(1, d).astype(F32), w_all=w_all, bias_all=bias_all, bd=bd,
        qw=jnp.tile(q_norm_w, FOX_HEADS).reshape(1, D_FOX).astype(F32),
        kw=jnp.tile(k_norm_w, FOX_HEADS).reshape(1, D_FOX).astype(F32),
        conv_w=conv_w.astype(F32), conv_b=conv_b.reshape(1, D_CONV).astype(F32), alog_all=alog_all.astype(F32),
        dskip=jnp.repeat(D_skip, SSD_HEAD_DIM).reshape(1, D_SSM).astype(F32),
        normw=ssd_norm_w.reshape(1, D_SSM).astype(F32), tri=tri,
        wo_s=w_out[:D_SSM].astype(BF16), wo_f=w_out[D_SSM:].astype(BF16),
        g2=norm2_w.reshape(1, d).astype(F32), w_up=w_up.astype(BF16), w_dn=w_down.astype(BF16),
    )


def _pick_tile(n, pref):
    t = min(pref, n)
    while n % t:
        t //= 2
    return t


def _trunk(x, conv_prev, ssm_prev, w, fox_fn, *, with_cumsum):
    b, l, d = x.shape
    m = b * l
    x2d = x.reshape(m, d)
    tm = _pick_tile(l if with_cumsum else m, 512)
    outs = _in_proj(x2d, w["g1"], w["w_all"], w["bias_all"], w["bd"], w["qw"], w["kw"],
                    tm=tm, seq_len=l, with_cumsum=with_cumsum)
    z, xbc, dt, q, kf, kb, vf, vb, small = outs[:9]
    conv_prev8 = jnp.concatenate(
        [jnp.zeros((b, SUBLANES - (CONV_W - 1), D_CONV), F32), conv_prev.astype(F32)], axis=1)
    y_ssd, st_out, conv_out = _ssd(xbc, dt, small, z, conv_prev8, _state_to_kernel_layout(ssm_prev.astype(F32)),
                                   w["conv_w"], w["conv_b"], w["alog_all"], w["dskip"], w["normw"], w["tri"],
                                   batch=b, seq_len=l)
    y_fox = fox_fn(q, kb, vb, small, outs[9] if with_cumsum else None, tm)
    y = _out_mlp(x2d, y_ssd, y_fox, w["wo_s"], w["wo_f"], w["g2"], w["w_up"], w["w_dn"], tm=_pick_tile(m, 512))
    logf = small[:, SMALL_F:SMALL_F + FOX_HEADS].reshape(b, l, FOX_HEADS)
    states = (kf.reshape(b, l, FOX_HEADS, FOX_HEAD_DIM), vf.reshape(b, l, FOX_HEADS, FOX_HEAD_DIM), logf,
              _state_from_kernel_layout(st_out), conv_out[:, SUBLANES - (CONV_W - 1):, :])
    return y.reshape(b, l, d), states


def kernel(x_prompt, x_sample, cache_k, cache_v, cache_logf, state_ssm, state_conv, norm1_w, w_in, conv_w, conv_b, dt_bias, A_log, D_skip, ssd_norm_w, f_bias, q_norm_w, k_norm_w, w_out, norm2_w, w_up, w_down):
    depth = w_in.shape[0]
    b_p, l_p, _ = x_prompt.shape
    b_s, l_s, _ = x_sample.shape
    past = cache_k.shape[2]
    y_prompt, y_sample = x_prompt, x_sample
    p_states, s_states = [], []
    for layer in range(depth):
        w = _layer_weights(norm1_w[layer], w_in[layer], conv_w[layer], conv_b[layer], dt_bias[layer], A_log[layer],
                           D_skip[layer], ssd_norm_w[layer], f_bias[layer], q_norm_w[layer], k_norm_w[layer],
                           w_out[layer], norm2_w[layer], w_up[layer], w_down[layer])

        def fox_prompt_fn(q, kb, vb, small, ct, tm):
            return _fox_prompt(q, kb, vb, ct, batch=b_p, seq_len=l_p, t=tm)

        def fox_sample_fn(q, kb, vb, small, ct, tm, layer=layer):
            lf_new = small[:, SMALL_F:SMALL_F + FOX_HEADS].reshape(b_s, l_s, FOX_HEADS).transpose(0, 2, 1)
            lf_new = jnp.pad(lf_new, ((0, 0), (0, 0), (0, LANES - l_s)))
            lf_cache = cache_logf[layer].astype(F32).transpose(0, 2, 1)
            return _fox_sample(q, cache_k[layer].reshape(b_s, past, D_FOX), cache_v[layer].reshape(b_s, past, D_FOX),
                               lf_cache, kb, vb, lf_new, batch=b_s, tn=l_s, past=past, tk=_pick_tile(past, 1024))

        conv0 = jnp.zeros((b_p, CONV_W - 1, D_CONV), F32)
        ssm0 = jnp.zeros((b_p, SSD_HEADS, SSD_HEAD_DIM, D_STATE), F32)
        y_prompt, st_p = _trunk(y_prompt, conv0, ssm0, w, fox_prompt_fn, with_cumsum=True)
        y_sample, st_s = _trunk(y_sample, state_conv[layer], state_ssm[layer], w, fox_sample_fn, with_cumsum=False)
        p_states.append(st_p)
        s_states.append(st_s)
    stack = lambda states, i: jnp.stack([s[i] for s in states])
    return (y_prompt, y_sample,
            stack(p_states, 0), stack(p_states, 1), stack(p_states, 2), stack(p_states, 3), stack(p_states, 4),
            stack(s_states, 0), stack(s_states, 1), stack(s_states, 2), stack(s_states, 3), stack(s_states, 4))
```

```python
import functools
import math

import jax
import jax.numpy as jnp
from jax import lax
from jax.experimental import pallas as pl
from jax.experimental.pallas import tpu as pltpu

F32 = jnp.float32
BF16 = jnp.bfloat16

EPS = 1e-6
SSD_HEADS = 16
SSD_HEAD_DIM = 64
SSD_GROUPS = 2
HEADS_PER_GROUP = SSD_HEADS // SSD_GROUPS
D_STATE = 128
CONV_W = 4
D_SSM = SSD_HEADS * SSD_HEAD_DIM
D_BC = SSD_GROUPS * D_STATE
D_CONV = D_SSM + 2 * D_BC
GROUP_W = D_SSM // SSD_GROUPS
FOX_HEADS = 8
FOX_HEAD_DIM = 64
D_FOX = FOX_HEADS * FOX_HEAD_DIM

LANES = 128
SUBLANES = 8
SSD_CHUNK = 128
VMEM_LIMIT = 56 * 1024 * 1024

C_Z = 0
C_XBC = C_Z + D_SSM
C_DT = C_XBC + D_CONV
C_Q = C_DT + D_SSM
C_K = C_Q + D_FOX
C_V = C_K + D_FOX
C_SMALL = C_V + D_FOX
N_PROJ = C_SMALL + LANES
SMALL_DT = 0
SMALL_F = SSD_HEADS
KAUG_W = 2 * LANES
AUG_TERMS = 3
LOG2E = math.log2(math.e)
Q_SCALE = FOX_HEAD_DIM ** -0.5 * LOG2E
VT_ROWS = 2 * FOX_HEAD_DIM


def _softplus(x):
    return jnp.maximum(x, 0.0) + jnp.log1p(jnp.exp(-jnp.abs(x)))


def _silu(x):
    return x * (1.0 / (1.0 + jnp.exp(-x)))


def _lane_cumsum(x):
    n = x.shape[-1]
    lane = lax.broadcasted_iota(jnp.int32, x.shape, x.ndim - 1)
    k = 1
    while k < n:
        x = x + jnp.where(lane >= k, pltpu.roll(x, k, axis=x.ndim - 1), 0.0)
        k *= 2
    return x


def _const_spec(shape):
    zeros = (0,) * len(shape)
    return pl.BlockSpec(shape, lambda *_: zeros, pipeline_mode=pl.Buffered(1))


def _in_proj_kernel(x_ref, g_ref, w_ref, bias_ref, bd_ref, qw_ref, kw_ref, *rest, tiles_per_seq, prompt):
    if prompt:
        (place_ref, z_ref, xbc_ref, dt_ref, q_ref, kf_ref, vf_ref, small_ref, kaug_ref, vt_ref, carry_ref) = rest
    else:
        (z_ref, xbc_ref, dt_ref, q_ref, kf_ref, vf_ref, small_ref, kb_ref, vb_ref) = rest
    x = x_ref[...]
    ms = jnp.mean(x * x, axis=-1, keepdims=True)
    h = (x * lax.rsqrt(ms + EPS) * g_ref[...]).astype(BF16)

    def proj(lo, hi):
        return jnp.dot(h, w_ref[:, lo:hi], preferred_element_type=F32)

    z_ref[...] = proj(C_Z, C_XBC).astype(BF16)
    xbc_ref[...] = proj(C_XBC, C_DT).astype(BF16)
    dt_ref[...] = _softplus(proj(C_DT, C_Q) + bias_ref[:, 0:D_SSM])

    def head_rms(t, w):
        ss = jnp.dot((t * t).astype(BF16), bd_ref[...], preferred_element_type=F32)
        return t * lax.rsqrt(ss * (1.0 / FOX_HEAD_DIM) + EPS) * w

    qn = head_rms(proj(C_Q, C_K), qw_ref[...])
    q_ref[...] = (qn * Q_SCALE).astype(BF16) if prompt else (qn * (FOX_HEAD_DIM ** -0.5)).astype(BF16)
    kn = head_rms(proj(C_K, C_V), kw_ref[...])
    kf_ref[...] = kn
    v = proj(C_V, C_SMALL)
    vf_ref[...] = v

    sm = proj(C_SMALL, N_PROJ) + bias_ref[:, D_SSM:D_SSM + LANES]
    lane = lax.broadcasted_iota(jnp.int32, sm.shape, 1)
    sm = jnp.where(lane < SMALL_F, _softplus(sm), -_softplus(-sm))
    small_ref[...] = sm

    if not prompt:
        kb_ref[...] = kn.astype(BF16)
        vb_ref[...] = v.astype(BF16)
        return

    @pl.when(pl.program_id(0) % tiles_per_seq == 0)
    def _():
        carry_ref[...] = jnp.zeros_like(carry_ref)

    cs_t = _lane_cumsum(sm.T) + carry_ref[:, 0:1]
    carry_ref[...] = jnp.broadcast_to(cs_t[:, -1:], carry_ref.shape)
    negc = cs_t.T * (-LOG2E)
    hi = negc.astype(BF16)
    r1 = negc - hi.astype(F32)
    mid = r1.astype(BF16)
    lo = (r1 - mid.astype(F32)).astype(BF16)
    aug = (jnp.dot(hi, place_ref[0], preferred_element_type=F32)
           + jnp.dot(mid, place_ref[1], preferred_element_type=F32)
           + jnp.dot(lo, place_ref[2], preferred_element_type=F32)).astype(BF16)
    kb = kn.astype(BF16)
    for p in range(FOX_HEADS // 2):
        kaug_ref[:, p * KAUG_W:p * KAUG_W + LANES] = kb[:, p * LANES:(p + 1) * LANES]
        kaug_ref[:, p * KAUG_W + LANES:(p + 1) * KAUG_W] = aug[:, p * LANES:(p + 1) * LANES]
    v_t = v.T
    tm = v_t.shape[1]
    sub = lax.broadcasted_iota(jnp.int32, (VT_ROWS - FOX_HEAD_DIM, tm), 0)
    ones_then_zeros = jnp.where(sub == 0, 1.0, 0.0)
    for hd in range(FOX_HEADS):
        blk = jnp.concatenate([v_t[hd * FOX_HEAD_DIM:(hd + 1) * FOX_HEAD_DIM], ones_then_zeros], axis=0)
        vt_ref[0, hd * VT_ROWS:(hd + 1) * VT_ROWS, :] = blk.astype(BF16)


def _in_proj(x2d, g, w_all, bias_all, bd, qw, kw, place, *, tm, seq_len, prompt):
    m, d = x2d.shape
    nt = m // tm
    row = lambda i: (i, 0)
    out_shape = [
        jax.ShapeDtypeStruct((m, D_SSM), BF16),
        jax.ShapeDtypeStruct((m, D_CONV), BF16),
        jax.ShapeDtypeStruct((m, D_SSM), F32),
        jax.ShapeDtypeStruct((m, D_FOX), BF16),
        jax.ShapeDtypeStruct((m, D_FOX), F32),
        jax.ShapeDtypeStruct((m, D_FOX), F32),
        jax.ShapeDtypeStruct((m, LANES), F32),
    ]
    out_specs = [
        pl.BlockSpec((tm, D_SSM), row), pl.BlockSpec((tm, D_CONV), row), pl.BlockSpec((tm, D_SSM), row),
        pl.BlockSpec((tm, D_FOX), row), pl.BlockSpec((tm, D_FOX), row), pl.BlockSpec((tm, D_FOX), row),
        pl.BlockSpec((tm, LANES), row),
    ]
    in_specs = [
        pl.BlockSpec((tm, d), row),
        _const_spec((1, d)), _const_spec((d, N_PROJ)), _const_spec((1, D_SSM + LANES)),
        _const_spec((D_FOX, D_FOX)), _const_spec((1, D_FOX)), _const_spec((1, D_FOX)),
    ]
    args = [x2d, g, w_all, bias_all, bd, qw, kw]
    scratch = []
    if prompt:
        in_specs.append(_const_spec(place.shape))
        args.append(place)
        out_shape += [jax.ShapeDtypeStruct((m, (FOX_HEADS // 2) * KAUG_W), BF16),
                      jax.ShapeDtypeStruct((nt, FOX_HEADS * VT_ROWS, tm), BF16)]
        out_specs += [pl.BlockSpec((tm, (FOX_HEADS // 2) * KAUG_W), row),
                      pl.BlockSpec((1, FOX_HEADS * VT_ROWS, tm), lambda i: (i, 0, 0))]
        scratch.append(pltpu.VMEM((LANES, LANES), F32))
    else:
        out_shape += [jax.ShapeDtypeStruct((m, D_FOX), BF16), jax.ShapeDtypeStruct((m, D_FOX), BF16)]
        out_specs += [pl.BlockSpec((tm, D_FOX), row), pl.BlockSpec((tm, D_FOX), row)]
    kern = functools.partial(_in_proj_kernel, tiles_per_seq=max(seq_len // tm, 1), prompt=prompt)
    return pl.pallas_call(
        kern,
        grid=(nt,),
        in_specs=in_specs,
        out_specs=out_specs,
        out_shape=out_shape,
        scratch_shapes=scratch,
        compiler_params=pltpu.CompilerParams(dimension_semantics=("arbitrary",), vmem_limit_bytes=VMEM_LIMIT),
        name="in_proj",
    )(*args)


def _pad_rows(x, rows):
    if x.shape[0] == rows:
        return x
    return jnp.concatenate([x, jnp.zeros((rows - x.shape[0],) + x.shape[1:], x.dtype)], axis=0)


def _ssd_kernel(xbc_ref, dt_ref, small_ref, z_ref, convprev_ref, st0_ref,
                convw_ref, convb_ref, alog_ref, dskip_ref, normw_ref, tri_ref,
                y_ref, stout_ref, convout_ref,
                xpad_scr, st_scr, ydiag_scr, *, lc):
    c = pl.program_id(1)
    P = SSD_CHUNK

    @pl.when(c == 0)
    def _():
        xpad_scr[0:SUBLANES] = convprev_ref[0]
        st_scr[...] = st0_ref[0]

    xpad_scr[SUBLANES:SUBLANES + lc] = xbc_ref[...].astype(F32)
    conv = convb_ref[...]
    for k in range(CONV_W):
        off = SUBLANES - (CONV_W - 1) + k
        conv = conv + convw_ref[k:k + 1, :] * xpad_scr[off:off + lc]
    last = xpad_scr[lc:lc + SUBLANES]
    xpad_scr[0:SUBLANES] = last
    convout_ref[0] = last

    act = _pad_rows(_silu(conv), P)
    dt = _pad_rows(dt_ref[...], P)
    dt_c = _pad_rows(small_ref[...], P)
    xs = act[:, :D_SSM]
    xdt = xs * dt

    a_all = -jnp.exp(alog_ref[...])
    da = jnp.concatenate([dt, dt_c], axis=1) * a_all
    da_hi = da.astype(BF16)
    da_lo = (da - da_hi.astype(F32)).astype(BF16)
    tri = tri_ref[...]
    acs_all = (jnp.dot(tri, da_hi, preferred_element_type=F32)
               + jnp.dot(tri, da_lo, preferred_element_type=F32))
    acs = acs_all[:, :D_SSM]
    acs_t = acs_all[:, D_SSM:].T
    a_last = acs[P - 1:P, :]
    ea = jnp.exp(acs)
    xt = (xdt * jnp.exp(a_last - acs)).astype(BF16)
    sd = jnp.exp(a_last)

    rowi = lax.broadcasted_iota(jnp.int32, (P, P), 0)
    coli = lax.broadcasted_iota(jnp.int32, (P, P), 1)
    tril = coli <= rowi
    lane = lax.broadcasted_iota(jnp.int32, (P, LANES), 1)
    first_half = lane < SSD_HEAD_DIM

    yoff = []
    for g in range(SSD_GROUPS):
        bg = act[:, D_SSM + g * D_STATE:D_SSM + (g + 1) * D_STATE]
        cg = act[:, D_SSM + D_BC + g * D_STATE:D_SSM + D_BC + (g + 1) * D_STATE]
        bgb = bg.astype(BF16)
        cgb = cg.astype(BF16)
        cb = lax.dot_general(cgb, bgb, (((1,), (1,)), ((), ())), preferred_element_type=F32)
        for pr in range(HEADS_PER_GROUP // 2):
            lo = g * GROUP_W + pr * LANES
            h0 = g * HEADS_PER_GROUP + 2 * pr
            slab = acs[:, lo:lo + LANES]
            rolled = pltpu.roll(slab, SSD_HEAD_DIM, axis=1)
            xpair = xdt[:, lo:lo + LANES]
            yhead = []
            for j in range(2):
                col = jnp.where(first_half, slab, rolled) if j == 0 else jnp.where(first_half, rolled, slab)
                seg = col - acs_t[h0 + j:h0 + j + 1, :]
                dec = jnp.exp(jnp.where(tril, seg, -jnp.inf))
                mh = (cb * dec).astype(BF16)
                keep = first_half if j == 0 else jnp.logical_not(first_half)
                xm = jnp.where(keep, xpair, 0.0).astype(BF16)
                yhead.append(jnp.dot(mh, xm, preferred_element_type=F32))
            ydiag_scr[:, lo:lo + LANES] = yhead[0] + yhead[1]
        gs = slice(g * GROUP_W, (g + 1) * GROUP_W)
        st = st_scr[g]
        yoff.append(jnp.dot(cgb, st.astype(BF16), preferred_element_type=F32) * ea[:, gs])
        st_scr[g] = st * sd[:, gs] + jnp.dot(bg.T.astype(BF16), xt[:, gs], preferred_element_type=F32)

    y = ydiag_scr[...] + jnp.concatenate(yoff, axis=1) + xs * dskip_ref[...]
    y = y[:lc] * _silu(z_ref[...].astype(F32))
    outs = []
    for g in range(SSD_GROUPS):
        yg = y[:, g * GROUP_W:(g + 1) * GROUP_W]
        outs.append(yg * lax.rsqrt(jnp.mean(yg * yg, axis=-1, keepdims=True) + EPS))
    y_ref[...] = (jnp.concatenate(outs, axis=1) * normw_ref[...]).astype(BF16)
    stout_ref[0] = st_scr[...]


def _ssd(xbc, dt, small, z, conv_prev8, st0, conv_w, conv_b, alog_all, dskip, normw, tri, *, batch, seq_len):
    lc = min(SSD_CHUNK, seq_len)
    nc = seq_len // lc
    blk = lambda b, c: (b * nc + c, 0)
    per_b3 = lambda b, c: (b, 0, 0)
    per_b4 = lambda b, c: (b, 0, 0, 0)
    m = batch * seq_len
    return pl.pallas_call(
        functools.partial(_ssd_kernel, lc=lc),
        grid=(batch, nc),
        in_specs=[
            pl.BlockSpec((lc, D_CONV), blk), pl.BlockSpec((lc, D_SSM), blk), pl.BlockSpec((lc, LANES), blk),
            pl.BlockSpec((lc, D_SSM), blk),
            pl.BlockSpec((1, SUBLANES, D_CONV), per_b3),
            pl.BlockSpec((1, SSD_GROUPS, D_STATE, GROUP_W), per_b4),
            _const_spec((CONV_W, D_CONV)), _const_spec((1, D_CONV)), _const_spec((1, D_SSM + LANES)),
            _const_spec((1, D_SSM)), _const_spec((1, D_SSM)), _const_spec((SSD_CHUNK, SSD_CHUNK)),
        ],
        out_specs=[
            pl.BlockSpec((lc, D_SSM), blk),
            pl.BlockSpec((1, SSD_GROUPS, D_STATE, GROUP_W), per_b4),
            pl.BlockSpec((1, SUBLANES, D_CONV), per_b3),
        ],
        out_shape=[
            jax.ShapeDtypeStruct((m, D_SSM), BF16),
            jax.ShapeDtypeStruct((batch, SSD_GROUPS, D_STATE, GROUP_W), F32),
            jax.ShapeDtypeStruct((batch, SUBLANES, D_CONV), F32),
        ],
        scratch_shapes=[
            pltpu.VMEM((lc + SUBLANES, D_CONV), F32),
            pltpu.VMEM((SSD_GROUPS, D_STATE, GROUP_W), F32),
            pltpu.VMEM((SSD_CHUNK, D_SSM), F32),
        ],
        compiler_params=pltpu.CompilerParams(dimension_semantics=("arbitrary", "arbitrary"),
                                             vmem_limit_bytes=VMEM_LIMIT),
        name="ssd",
    )(xbc, dt, small, z, conv_prev8, st0, conv_w, conv_b, alog_all, dskip, normw, tri)


def _fox_prompt_kernel(q_ref, ka_ref, vt_ref, o_ref, qa_scr, sta_scr, stb_scr, acc_scr, *, t):
    qi = pl.program_id(2)
    lane = lax.broadcasted_iota(jnp.int32, (t, LANES), 1)
    q = q_ref[...]
    for j in range(2):
        own = (lane < FOX_HEAD_DIM) if j == 0 else (lane >= FOX_HEAD_DIM)
        pick = jnp.where(lane >= AUG_TERMS * j, jnp.where(lane < AUG_TERMS * (j + 1), 1.0, 0.0), 0.0)
        qa_scr[j] = jnp.concatenate([jnp.where(own, q, jnp.zeros_like(q)), pick.astype(BF16)], axis=1)
    acc_scr[...] = jnp.zeros(acc_scr.shape, F32)

    def logits(ki, st_scr):
        start = pl.multiple_of(ki * t, t)
        ka = ka_ref[pl.ds(start, t), :]
        for j in range(2):
            st_scr[j] = lax.dot_general(ka, qa_scr[j], (((1,), (1,)), ((), ())),
                                        preferred_element_type=F32)

    def consume(ki, st_scr, m_prev, masked):
        out = []
        for j in range(2):
            st = st_scr[j]
            if masked:
                rowi = lax.broadcasted_iota(jnp.int32, (t, t), 0)
                coli = lax.broadcasted_iota(jnp.int32, (t, t), 1)
                st = jnp.where(rowi <= coli, st, -jnp.inf)
            m_new = jnp.maximum(m_prev[j], jnp.max(st, axis=0, keepdims=True))
            alpha = jnp.exp2(m_prev[j] - m_new)
            pexp = jnp.exp2(st - m_new).astype(BF16)
            vt = vt_ref[ki, j * VT_ROWS:(j + 1) * VT_ROWS, :]
            acc_scr[j] = alpha * acc_scr[j] + jnp.dot(vt, pexp, preferred_element_type=F32)
            out.append(m_new)
        return tuple(out)

    logits(0, sta_scr)

    def body(i, m):
        logits(2 * i + 1, stb_scr)
        m = consume(2 * i, sta_scr, m, False)
        logits(2 * i + 2, sta_scr)
        return consume(2 * i + 1, stb_scr, m, False)

    m0 = tuple(jnp.full((1, t), -jnp.inf, F32) for _ in range(2))
    m = lax.fori_loop(0, qi // 2, body, m0)
    odd = lax.rem(qi, 2) == 1

    @pl.when(jnp.logical_not(odd))
    def _():
        consume(qi, sta_scr, m, True)

    @pl.when(odd)
    def _():
        logits(qi, stb_scr)
        m1 = consume(qi - 1, sta_scr, m, False)
        consume(qi, stb_scr, m1, True)

    halves = []
    for j in range(2):
        acc = acc_scr[j]
        halves.append(acc[:FOX_HEAD_DIM] * (1.0 / acc[FOX_HEAD_DIM:FOX_HEAD_DIM + 1]))
    o_ref[...] = jnp.concatenate(halves, axis=0).T.astype(BF16)


def _fox_prompt(q, kaug, vt, *, batch, seq_len, t):
    nq = seq_len // t
    m = batch * seq_len
    return pl.pallas_call(
        functools.partial(_fox_prompt_kernel, t=t),
        grid=(batch, FOX_HEADS // 2, nq),
        in_specs=[
            pl.BlockSpec((t, LANES), lambda b, p, i: (b * nq + i, p)),
            pl.BlockSpec((seq_len, KAUG_W), lambda b, p, i: (b, p)),
            pl.BlockSpec((nq, 2 * VT_ROWS, t), lambda b, p, i: (b, p, 0)),
        ],
        out_specs=pl.BlockSpec((t, LANES), lambda b, p, i: (b * nq + i, p)),
        out_shape=jax.ShapeDtypeStruct((m, D_FOX), BF16),
        scratch_shapes=[
            pltpu.VMEM((2, t, KAUG_W), BF16),
            pltpu.VMEM((2, t, t), F32),
            pltpu.VMEM((2, t, t), F32),
            pltpu.VMEM((2, VT_ROWS, t), F32),
        ],
        compiler_params=pltpu.CompilerParams(dimension_semantics=("arbitrary", "arbitrary", "arbitrary"),
                                             vmem_limit_bytes=VMEM_LIMIT),
        name="fox_prompt",
    )(q, kaug, vt)


def _fox_sample_kernel(q_ref, kc_ref, vc_ref, lfc_ref, kn_ref, vn_ref, lfn_ref, own_ref, qpos_ref, o_ref,
                       qbd_scr, m_scr, l_scr, acc_scr, carry_scr, *, tn):
    j = pl.program_id(1)
    nk = pl.num_programs(1)
    rows = FOX_HEADS * tn
    own = own_ref[...] > 0.0

    @pl.when(j == 0)
    def _():
        qt = jnp.concatenate([q_ref[...]] * FOX_HEADS, axis=0)
        qbd_scr[...] = jnp.where(own, qt, jnp.zeros_like(qt))
        m_scr[...] = jnp.full(m_scr.shape, -jnp.inf, F32)
        l_scr[...] = jnp.zeros(l_scr.shape, F32)
        acc_scr[...] = jnp.zeros(acc_scr.shape, F32)
        carry_scr[...] = jnp.zeros(carry_scr.shape, F32)

    def expand(ct):
        n = ct.shape[-1]
        return jnp.concatenate([jnp.broadcast_to(ct[h:h + 1], (tn, n)) for h in range(FOX_HEADS)], axis=0)

    def update(s, vals):
        m_prev = m_scr[...]
        m_new = jnp.maximum(m_prev, jnp.max(s, axis=-1, keepdims=True))
        alpha = jnp.exp(m_prev - m_new)
        pexp = jnp.exp(s - m_new)
        l_scr[...] = alpha * l_scr[...] + jnp.sum(pexp, axis=-1, keepdims=True)
        acc_scr[...] = alpha * acc_scr[...] + jnp.dot(pexp.astype(BF16), vals, preferred_element_type=F32)
        m_scr[...] = m_new

    nt_dims = (((1,), (1,)), ((), ()))
    ct = _lane_cumsum(lfc_ref[0]) + carry_scr[:, 0:1]
    carry_scr[...] = jnp.broadcast_to(ct[:, -1:], carry_scr.shape)
    kb = kc_ref[0].astype(BF16)
    s = lax.dot_general(qbd_scr[...], kb, nt_dims, preferred_element_type=F32) - expand(ct)
    update(s, vc_ref[0].astype(BF16))

    @pl.when(j == nk - 1)
    def _():
        cn = _lane_cumsum(lfn_ref[0]) + carry_scr[:, 0:1]
        kn = _pad_rows(kn_ref[...], LANES)
        vn = _pad_rows(vn_ref[...], LANES)
        sn = lax.dot_general(qbd_scr[...], kn, nt_dims, preferred_element_type=F32) - expand(cn)
        cidx = lax.broadcasted_iota(jnp.int32, (rows, LANES), 1)
        sn = jnp.where(cidx <= qpos_ref[...], sn, -jnp.inf)
        update(sn, vn)
        o = acc_scr[...] / l_scr[...]
        o = jnp.where(own, o, 0.0)
        out = o[0:tn]
        for h in range(1, FOX_HEADS):
            out = out + o[h * tn:(h + 1) * tn]
        o_ref[...] = out.astype(BF16)


def _fox_sample(q, kc, vc, lfc_t, kn, vn, lfn_t, *, batch, tn, past, tk):
    nk = past // tk
    rows = FOX_HEADS * tn
    r = jnp.arange(rows)
    own = ((r // tn)[:, None] == (jnp.arange(D_FOX) // FOX_HEAD_DIM)[None, :]).astype(F32)
    qpos = (r % tn).astype(jnp.int32).reshape(rows, 1)
    return pl.pallas_call(
        functools.partial(_fox_sample_kernel, tn=tn),
        grid=(batch, nk),
        in_specs=[
            pl.BlockSpec((tn, D_FOX), lambda b, j: (b, 0)),
            pl.BlockSpec((1, tk, D_FOX), lambda b, j: (b, j, 0)),
            pl.BlockSpec((1, tk, D_FOX), lambda b, j: (b, j, 0)),
            pl.BlockSpec((1, FOX_HEADS, tk), lambda b, j: (b, 0, j)),
            pl.BlockSpec((tn, D_FOX), lambda b, j: (b, 0)),
            pl.BlockSpec((tn, D_FOX), lambda b, j: (b, 0)),
            pl.BlockSpec((1, FOX_HEADS, LANES), lambda b, j: (b, 0, 0)),
            _const_spec((rows, D_FOX)), _const_spec((rows, 1)),
        ],
        out_specs=pl.BlockSpec((tn, D_FOX), lambda b, j: (b, 0)),
        out_shape=jax.ShapeDtypeStruct((batch * tn, D_FOX), BF16),
        scratch_shapes=[
            pltpu.VMEM((rows, D_FOX), BF16),
            pltpu.VMEM((rows, 1), F32),
            pltpu.VMEM((rows, 1), F32),
            pltpu.VMEM((rows, D_FOX), F32),
            pltpu.VMEM((FOX_HEADS, LANES), F32),
        ],
        compiler_params=pltpu.CompilerParams(dimension_semantics=("arbitrary", "arbitrary"),
                                             vmem_limit_bytes=VMEM_LIMIT),
        name="fox_sample",
    )(q, kc, vc, lfc_t, kn, vn, lfn_t, own, qpos)


def _out_mlp_kernel(x_ref, ys_ref, yf_ref, wos_ref, wof_ref, g_ref, wup_ref, wdn_ref, o_ref):
    x1 = (x_ref[...]
          + jnp.dot(ys_ref[...], wos_ref[...], preferred_element_type=F32)
          + jnp.dot(yf_ref[...], wof_ref[...], preferred_element_type=F32))
    ms = jnp.mean(x1 * x1, axis=-1, keepdims=True)
    h = (x1 * lax.rsqrt(ms + EPS) * g_ref[...]).astype(BF16)
    up = jnp.dot(h, wup_ref[...], preferred_element_type=F32)
    a = jnp.square(jnp.maximum(up, 0.0)).astype(BF16)
    o_ref[...] = x1 + jnp.dot(a, wdn_ref[...], preferred_element_type=F32)


def _out_mlp(x2d, ys, yf, wo_s, wo_f, g2, w_up, w_dn, *, tm):
    m, d = x2d.shape
    d_ff = w_up.shape[1]
    row = lambda i: (i, 0)
    return pl.pallas_call(
        _out_mlp_kernel,
        grid=(m // tm,),
        in_specs=[
            pl.BlockSpec((tm, d), row), pl.BlockSpec((tm, D_SSM), row), pl.BlockSpec((tm, D_FOX), row),
            _const_spec((D_SSM, d)), _const_spec((D_FOX, d)), _const_spec((1, d)),
            _const_spec((d, d_ff)), _const_spec((d_ff, d)),
        ],
        out_specs=pl.BlockSpec((tm, d), row),
        out_shape=jax.ShapeDtypeStruct((m, d), F32),
        compiler_params=pltpu.CompilerParams(dimension_semantics=("arbitrary",), vmem_limit_bytes=VMEM_LIMIT),
        name="out_mlp",
    )(x2d, ys, yf, wo_s, wo_f, g2, w_up, w_dn)


def _state_to_kernel_layout(st):
    b = st.shape[0]
    st = st.reshape(b, SSD_GROUPS, HEADS_PER_GROUP, SSD_HEAD_DIM, D_STATE)
    return st.transpose(0, 1, 4, 2, 3).reshape(b, SSD_GROUPS, D_STATE, GROUP_W)


def _state_from_kernel_layout(st):
    b = st.shape[0]
    st = st.reshape(b, SSD_GROUPS, D_STATE, HEADS_PER_GROUP, SSD_HEAD_DIM)
    return st.transpose(0, 1, 3, 4, 2).reshape(b, SSD_HEADS, SSD_HEAD_DIM, D_STATE)


def _layer_weights(norm1_w, w_in, conv_w, conv_b, dt_bias, A_log, D_skip, ssd_norm_w, f_bias,
                   q_norm_w, k_norm_w, w_out, norm2_w, w_up, w_down):
    d = w_in.shape[0]
    splits = [D_SSM, D_CONV, SSD_HEADS, D_FOX, D_FOX, D_FOX]
    idx = [sum(splits[:i + 1]) for i in range(len(splits))]
    w_z, w_xbc, w_dt, w_q, w_k, w_v, w_f = jnp.split(w_in, idx, axis=1)
    pad = jnp.zeros((d, LANES - SSD_HEADS - FOX_HEADS), w_in.dtype)
    w_all = jnp.concatenate(
        [w_z, w_xbc, jnp.repeat(w_dt, SSD_HEAD_DIM, axis=1), w_q, w_k, w_v, w_dt, w_f, pad], axis=1).astype(BF16)
    bias_all = jnp.concatenate(
        [jnp.repeat(dt_bias, SSD_HEAD_DIM), dt_bias, f_bias,
         jnp.zeros((LANES - SSD_HEADS - FOX_HEADS,), F32)]).reshape(1, D_SSM + LANES).astype(F32)
    head_id = jnp.arange(D_FOX) // FOX_HEAD_DIM
    bd = (head_id[:, None] == head_id[None, :]).astype(BF16)
    alog_all = jnp.concatenate(
        [jnp.repeat(A_log, SSD_HEAD_DIM), A_log, jnp.zeros((LANES - SSD_HEADS,), F32)]).reshape(1, D_SSM + LANES)
    r = jnp.arange(SSD_CHUNK)
    tri = (r[None, :] <= r[:, None]).astype(BF16)
    hh = jnp.arange(FOX_HEADS)
    place = jnp.zeros((AUG_TERMS, LANES, D_FOX), F32)
    for t in range(AUG_TERMS):
        place = place.at[t, SMALL_F + hh, (hh // 2) * LANES + (hh % 2) * AUG_TERMS + t].set(1.0)
    place = place.astype(BF16)
    return dict(
        g1=norm1_w.reshape(1, d).astype(F32), w_all=w_all, bias_all=bias_all, bd=bd,
        qw=jnp.tile(q_norm_w, FOX_HEADS).reshape(1, D_FOX).astype(F32),
        kw=jnp.tile(k_norm_w, FOX_HEADS).reshape(1, D_FOX).astype(F32),
        conv_w=conv_w.astype(F32), conv_b=conv_b.reshape(1, D_CONV).astype(F32), alog_all=alog_all.astype(F32),
        dskip=jnp.repeat(D_skip, SSD_HEAD_DIM).reshape(1, D_SSM).astype(F32),
        normw=ssd_norm_w.reshape(1, D_SSM).astype(F32), tri=tri, place=place,
        wo_s=w_out[:D_SSM].astype(BF16), wo_f=w_out[D_SSM:].astype(BF16),
        g2=norm2_w.reshape(1, d).astype(F32), w_up=w_up.astype(BF16), w_dn=w_down.astype(BF16),
    )


def _pick_tile(n, pref):
    t = min(pref, n)
    while n % t:
        t //= 2
    return t


def _trunk(x, conv_prev, ssm_prev, w, fox_fn, *, prompt):
    b, l, d = x.shape
    m = b * l
    x2d = x.reshape(m, d)
    tm = _pick_tile(l if prompt else m, 512)
    z, xbc, dt, q, kf, vf, small, ka, va = _in_proj(
        x2d, w["g1"], w["w_all"], w["bias_all"], w["bd"], w["qw"], w["kw"], w["place"],
        tm=tm, seq_len=l, prompt=prompt)
    conv_prev8 = jnp.concatenate(
        [jnp.zeros((b, SUBLANES - (CONV_W - 1), D_CONV), F32), conv_prev.astype(F32)], axis=1)
    y_ssd, st_out, conv_out = _ssd(xbc, dt, small, z, conv_prev8, _state_to_kernel_layout(ssm_prev.astype(F32)),
                                   w["conv_w"], w["conv_b"], w["alog_all"], w["dskip"], w["normw"], w["tri"],
                                   batch=b, seq_len=l)
    y_fox = fox_fn(q, ka, va, small, tm)
    y = _out_mlp(x2d, y_ssd, y_fox, w["wo_s"], w["wo_f"], w["g2"], w["w_up"], w["w_dn"], tm=_pick_tile(m, 512))
    logf = small[:, SMALL_F:SMALL_F + FOX_HEADS].reshape(b, l, FOX_HEADS)
    states = (kf.reshape(b, l, FOX_HEADS, FOX_HEAD_DIM), vf.reshape(b, l, FOX_HEADS, FOX_HEAD_DIM), logf,
              _state_from_kernel_layout(st_out), conv_out[:, SUBLANES - (CONV_W - 1):, :])
    return y.reshape(b, l, d), states


def kernel(x_prompt, x_sample, cache_k, cache_v, cache_logf, state_ssm, state_conv, norm1_w, w_in, conv_w, conv_b, dt_bias, A_log, D_skip, ssd_norm_w, f_bias, q_norm_w, k_norm_w, w_out, norm2_w, w_up, w_down):
    depth = w_in.shape[0]
    b_p, l_p, _ = x_prompt.shape
    b_s, l_s, _ = x_sample.shape
    past = cache_k.shape[2]
    y_prompt, y_sample = x_prompt, x_sample
    p_states, s_states = [], []
    for layer in range(depth):
        w = _layer_weights(norm1_w[layer], w_in[layer], conv_w[layer], conv_b[layer], dt_bias[layer], A_log[layer],
                           D_skip[layer], ssd_norm_w[layer], f_bias[layer], q_norm_w[layer], k_norm_w[layer],
                           w_out[layer], norm2_w[layer], w_up[layer], w_down[layer])

        def fox_prompt_fn(q, kaug, vt, small, tm):
            return _fox_prompt(q, kaug, vt, batch=b_p, seq_len=l_p, t=tm)

        def fox_sample_fn(q, kb, vb, small, tm, layer=layer):
            lf_new = small[:, SMALL_F:SMALL_F + FOX_HEADS].reshape(b_s, l_s, FOX_HEADS).transpose(0, 2, 1)
            lf_new = jnp.pad(lf_new, ((0, 0), (0, 0), (0, LANES - l_s)))
            lf_cache = cache_logf[layer].astype(F32).transpose(0, 2, 1)
            return _fox_sample(q, cache_k[layer].reshape(b_s, past, D_FOX), cache_v[layer].reshape(b_s, past, D_FOX),
                               lf_cache, kb, vb, lf_new, batch=b_s, tn=l_s, past=past, tk=_pick_tile(past, 1024))

        conv0 = jnp.zeros((b_p, CONV_W - 1, D_CONV), F32)
        ssm0 = jnp.zeros((b_p, SSD_HEADS, SSD_HEAD_DIM, D_STATE), F32)
        y_prompt, st_p = _trunk(y_prompt, conv0, ssm0, w, fox_prompt_fn, prompt=True)
        y_sample, st_s = _trunk(y_sample, state_conv[layer], state_ssm[layer], w, fox_sample_fn, prompt=False)
        p_states.append(st_p)
        s_states.append(st_s)
    stack = lambda states, i: jnp.stack([s[i] for s in states])
    return (y_prompt, y_sample,
            stack(p_states, 0), stack(p_states, 1), stack(p_states, 2), stack(p_states, 3), stack(p_states, 4),
            stack(s_states, 0), stack(s_states, 1), stack(s_states, 2), stack(s_states, 3), stack(s_states, 4))
```

```python
import functools
import math

import jax
import jax.numpy as jnp
from jax import lax
from jax.experimental import pallas as pl
from jax.experimental.pallas import tpu as pltpu

F32 = jnp.float32
BF16 = jnp.bfloat16

EPS = 1e-6
SSD_HEADS = 16
SSD_HEAD_DIM = 64
SSD_GROUPS = 2
HEADS_PER_GROUP = SSD_HEADS // SSD_GROUPS
D_STATE = 128
CONV_W = 4
D_SSM = SSD_HEADS * SSD_HEAD_DIM
D_BC = SSD_GROUPS * D_STATE
D_CONV = D_SSM + 2 * D_BC
GROUP_W = D_SSM // SSD_GROUPS
FOX_HEADS = 8
FOX_HEAD_DIM = 64
D_FOX = FOX_HEADS * FOX_HEAD_DIM

LANES = 128
SUBLANES = 8
SSD_CHUNK = 128
PREV_ROWS = 16
VMEM_LIMIT = 56 * 1024 * 1024

C_Z = 0
C_XBC = C_Z + D_SSM
C_DT = C_XBC + D_CONV
C_Q = C_DT + D_SSM
C_K = C_Q + D_FOX
C_V = C_K + D_FOX
C_SMALL = C_V + D_FOX
N_PROJ = C_SMALL + LANES
SMALL_DT = 0
SMALL_F = SSD_HEADS
KAUG_W = 2 * LANES
AUG_TERMS = 3
LOG2E = math.log2(math.e)
Q_SCALE = FOX_HEAD_DIM ** -0.5 * LOG2E
VT_ROWS = 2 * FOX_HEAD_DIM


def _softplus(x):
    return jnp.maximum(x, 0.0) + jnp.log1p(jnp.exp(-jnp.abs(x)))


def _silu(x):
    return x * (1.0 / (1.0 + jnp.exp(-x)))


def _lane_cumsum(x):
    n = x.shape[-1]
    lane = lax.broadcasted_iota(jnp.int32, x.shape, x.ndim - 1)
    k = 1
    while k < n:
        x = x + jnp.where(lane >= k, pltpu.roll(x, k, axis=x.ndim - 1), 0.0)
        k *= 2
    return x


def _const_spec(shape):
    zeros = (0,) * len(shape)
    return pl.BlockSpec(shape, lambda *_: zeros, pipeline_mode=pl.Buffered(1))


def _in_proj_kernel(x_ref, g_ref, w_ref, bias_ref, bd_ref, qw_ref, kw_ref, *rest, tiles_per_seq, prompt):
    if prompt:
        (place_ref, z_ref, xbc_ref, dt_ref, q_ref, kf_ref, vf_ref, small_ref, kaug_ref, vt_ref, carry_ref) = rest
    else:
        (z_ref, xbc_ref, dt_ref, q_ref, kf_ref, vf_ref, small_ref, kb_ref, vb_ref) = rest
    x = x_ref[...]
    ms = jnp.mean(x * x, axis=-1, keepdims=True)
    h = (x * lax.rsqrt(ms + EPS) * g_ref[...]).astype(BF16)

    def proj(lo, hi):
        return jnp.dot(h, w_ref[:, lo:hi], preferred_element_type=F32)

    z_ref[...] = proj(C_Z, C_XBC).astype(BF16)
    sm = proj(C_SMALL, N_PROJ) + bias_ref[:, D_SSM:D_SSM + LANES]
    lane = lax.broadcasted_iota(jnp.int32, sm.shape, 1)
    sm = jnp.where(lane < SMALL_F, _softplus(sm), -_softplus(-sm))
    small_ref[...] = sm
    if prompt:
        @pl.when(pl.program_id(0) % tiles_per_seq == 0)
        def _():
            carry_ref[...] = jnp.zeros_like(carry_ref)

        cs_t = _lane_cumsum(sm.T) + carry_ref[:, 0:1]
        carry_ref[...] = jnp.broadcast_to(cs_t[:, -1:], carry_ref.shape)
        negc = cs_t.T * (-LOG2E)
        hi = negc.astype(BF16)
        r1 = negc - hi.astype(F32)
        mid = r1.astype(BF16)
        lo = (r1 - mid.astype(F32)).astype(BF16)

    xbc_ref[...] = proj(C_XBC, C_DT).astype(BF16)
    dt_ref[...] = _softplus(proj(C_DT, C_Q) + bias_ref[:, 0:D_SSM])

    def head_rms(t, w):
        ss = jnp.dot((t * t).astype(BF16), bd_ref[...], preferred_element_type=F32)
        return t * lax.rsqrt(ss * (1.0 / FOX_HEAD_DIM) + EPS) * w

    qn = head_rms(proj(C_Q, C_K), qw_ref[...])
    q_ref[...] = (qn * Q_SCALE).astype(BF16) if prompt else (qn * (FOX_HEAD_DIM ** -0.5)).astype(BF16)
    kn = head_rms(proj(C_K, C_V), kw_ref[...])
    v = proj(C_V, C_SMALL)
    kf_ref[...] = kn.reshape(kf_ref.shape)
    vf_ref[...] = v.reshape(vf_ref.shape)

    kb = kn.astype(BF16)
    if not prompt:
        kb_ref[...] = kb
        vb_ref[...] = v.astype(BF16)
        return

    for p in range(FOX_HEADS // 2):
        kaug_ref[:, p * KAUG_W:p * KAUG_W + LANES] = kb[:, p * LANES:(p + 1) * LANES]
    v_t = v.T
    tm = v_t.shape[1]
    sub = lax.broadcasted_iota(jnp.int32, (VT_ROWS - FOX_HEAD_DIM, tm), 0)
    ones_then_zeros = jnp.where(sub == 0, 1.0, 0.0)
    for hd in range(FOX_HEADS):
        blk = jnp.concatenate([v_t[hd * FOX_HEAD_DIM:(hd + 1) * FOX_HEAD_DIM], ones_then_zeros], axis=0)
        vt_ref[0, hd * VT_ROWS:(hd + 1) * VT_ROWS, :] = blk.astype(BF16)
    aug = (jnp.dot(hi, place_ref[0], preferred_element_type=F32)
           + jnp.dot(mid, place_ref[1], preferred_element_type=F32)
           + jnp.dot(lo, place_ref[2], preferred_element_type=F32)).astype(BF16)
    for p in range(FOX_HEADS // 2):
        kaug_ref[:, p * KAUG_W + LANES:(p + 1) * KAUG_W] = aug[:, p * LANES:(p + 1) * LANES]


def _in_proj(x2d, g, w_all, bias_all, bd, qw, kw, place, *, tm, seq_len, prompt):
    m, d = x2d.shape
    nt = m // tm
    row = lambda i: (i, 0)
    out_shape = [
        jax.ShapeDtypeStruct((m, D_SSM), BF16),
        jax.ShapeDtypeStruct((m, D_CONV), BF16),
        jax.ShapeDtypeStruct((m, D_SSM), F32),
        jax.ShapeDtypeStruct((m, D_FOX), BF16),
        jax.ShapeDtypeStruct((m, FOX_HEADS, FOX_HEAD_DIM), F32),
        jax.ShapeDtypeStruct((m, FOX_HEADS, FOX_HEAD_DIM), F32),
        jax.ShapeDtypeStruct((m, LANES), F32),
    ]
    row3 = lambda i: (i, 0, 0)
    out_specs = [
        pl.BlockSpec((tm, D_SSM), row), pl.BlockSpec((tm, D_CONV), row), pl.BlockSpec((tm, D_SSM), row),
        pl.BlockSpec((tm, D_FOX), row),
        pl.BlockSpec((tm, FOX_HEADS, FOX_HEAD_DIM), row3), pl.BlockSpec((tm, FOX_HEADS, FOX_HEAD_DIM), row3),
        pl.BlockSpec((tm, LANES), row),
    ]
    in_specs = [
        pl.BlockSpec((tm, d), row),
        _const_spec((1, d)), _const_spec((d, N_PROJ)), _const_spec((1, D_SSM + LANES)),
        _const_spec((D_FOX, D_FOX)), _const_spec((1, D_FOX)), _const_spec((1, D_FOX)),
    ]
    args = [x2d, g, w_all, bias_all, bd, qw, kw]
    scratch = []
    if prompt:
        in_specs.append(_const_spec(place.shape))
        args.append(place)
        out_shape += [jax.ShapeDtypeStruct((m, (FOX_HEADS // 2) * KAUG_W), BF16),
                      jax.ShapeDtypeStruct((nt, FOX_HEADS * VT_ROWS, tm), BF16)]
        out_specs += [pl.BlockSpec((tm, (FOX_HEADS // 2) * KAUG_W), row),
                      pl.BlockSpec((1, FOX_HEADS * VT_ROWS, tm), lambda i: (i, 0, 0))]
        scratch.append(pltpu.VMEM((LANES, LANES), F32))
    else:
        out_shape += [jax.ShapeDtypeStruct((m, D_FOX), BF16), jax.ShapeDtypeStruct((m, D_FOX), BF16)]
        out_specs += [pl.BlockSpec((tm, D_FOX), row), pl.BlockSpec((tm, D_FOX), row)]
    kern = functools.partial(_in_proj_kernel, tiles_per_seq=max(seq_len // tm, 1), prompt=prompt)
    return pl.pallas_call(
        kern,
        grid=(nt,),
        in_specs=in_specs,
        out_specs=out_specs,
        out_shape=out_shape,
        scratch_shapes=scratch,
        compiler_params=pltpu.CompilerParams(dimension_semantics=("arbitrary",), vmem_limit_bytes=VMEM_LIMIT),
        name="in_proj",
    )(*args)


def _pad_rows(x, rows):
    if x.shape[0] == rows:
        return x
    return jnp.concatenate([x, jnp.zeros((rows - x.shape[0],) + x.shape[1:], x.dtype)], axis=0)


def _ssd_kernel(xbc_ref, dt_ref, small_ref, z_ref, convprev_ref, st0_ref,
                convw_ref, convb_ref, alog_ref, dskip_ref, normw_ref, tri_ref, shift_ref,
                y_ref, stout_ref, convout_ref,
                prev_scr, st_scr, ydiag_scr, *, lc):
    c = pl.program_id(1)
    last_chunk = c == pl.num_programs(1) - 1
    P = SSD_CHUNK

    @pl.when(c == 0)
    def _():
        prev_scr[...] = convprev_ref[0]
        st_scr[...] = st0_ref[0]

    cur = xbc_ref[...]
    shifted = jnp.dot(shift_ref[...], jnp.concatenate([prev_scr[...], cur], axis=0),
                      preferred_element_type=F32)
    conv = convb_ref[...]
    for k in range(CONV_W):
        conv = conv + convw_ref[k:k + 1, :] * shifted[k * lc:(k + 1) * lc]
    tail = cur[lc - PREV_ROWS:lc]
    prev_scr[0:2 * PREV_ROWS] = jnp.zeros((2 * PREV_ROWS, D_CONV), BF16)
    prev_scr[2 * PREV_ROWS:3 * PREV_ROWS] = tail

    @pl.when(last_chunk)
    def _():
        convout_ref[0] = tail.astype(F32)[PREV_ROWS - SUBLANES:]

    act = _pad_rows(_silu(conv), P)
    dt = _pad_rows(dt_ref[...], P)
    dt_c = _pad_rows(small_ref[...], P)
    xs = act[:, :D_SSM]
    xdt = xs * dt

    a_all = -jnp.exp(alog_ref[...])
    da = jnp.concatenate([dt, dt_c], axis=1) * a_all
    da_hi = da.astype(BF16)
    da_lo = (da - da_hi.astype(F32)).astype(BF16)
    tri = tri_ref[...]
    acs_all = (jnp.dot(tri, da_hi, preferred_element_type=F32)
               + jnp.dot(tri, da_lo, preferred_element_type=F32))
    acs = acs_all[:, :D_SSM]
    acs_t = acs_all[:, D_SSM:].T
    a_last = acs[P - 1:P, :]
    ea = jnp.exp(acs)
    xt = (xdt * jnp.exp(a_last - acs)).astype(BF16)
    sd = jnp.exp(a_last)

    rowi = lax.broadcasted_iota(jnp.int32, (P, P), 0)
    coli = lax.broadcasted_iota(jnp.int32, (P, P), 1)
    tril = coli <= rowi
    lane = lax.broadcasted_iota(jnp.int32, (P, LANES), 1)
    first_half = lane < SSD_HEAD_DIM

    yoff = []
    for g in range(SSD_GROUPS):
        bg = act[:, D_SSM + g * D_STATE:D_SSM + (g + 1) * D_STATE]
        cg = act[:, D_SSM + D_BC + g * D_STATE:D_SSM + D_BC + (g + 1) * D_STATE]
        bgb = bg.astype(BF16)
        cgb = cg.astype(BF16)
        cb = lax.dot_general(cgb, bgb, (((1,), (1,)), ((), ())), preferred_element_type=F32)
        for pr in range(HEADS_PER_GROUP // 2):
            lo = g * GROUP_W + pr * LANES
            h0 = g * HEADS_PER_GROUP + 2 * pr
            slab = acs[:, lo:lo + LANES]
            rolled = pltpu.roll(slab, SSD_HEAD_DIM, axis=1)
            xpair = xdt[:, lo:lo + LANES]
            yhead = []
            for j in range(2):
                col = jnp.where(first_half, slab, rolled) if j == 0 else jnp.where(first_half, rolled, slab)
                seg = col - acs_t[h0 + j:h0 + j + 1, :]
                dec = jnp.exp(jnp.where(tril, seg, -jnp.inf))
                mh = (cb * dec).astype(BF16)
                keep = first_half if j == 0 else jnp.logical_not(first_half)
                xm = jnp.where(keep, xpair, 0.0).astype(BF16)
                yhead.append(jnp.dot(mh, xm, preferred_element_type=F32))
            ydiag_scr[:, lo:lo + LANES] = yhead[0] + yhead[1]
        gs = slice(g * GROUP_W, (g + 1) * GROUP_W)
        st = st_scr[g]
        yoff.append(jnp.dot(cgb, st.astype(BF16), preferred_element_type=F32) * ea[:, gs])
        st_scr[g] = st * sd[:, gs] + jnp.dot(bg.T.astype(BF16), xt[:, gs], preferred_element_type=F32)

    y = ydiag_scr[...] + jnp.concatenate(yoff, axis=1) + xs * dskip_ref[...]
    y = y[:lc] * _silu(z_ref[...].astype(F32))
    outs = []
    for g in range(SSD_GROUPS):
        yg = y[:, g * GROUP_W:(g + 1) * GROUP_W]
        outs.append(yg * lax.rsqrt(jnp.mean(yg * yg, axis=-1, keepdims=True) + EPS))
    y_ref[...] = (jnp.concatenate(outs, axis=1) * normw_ref[...]).astype(BF16)

    @pl.when(last_chunk)
    def _():
        stout_ref[0] = st_scr[...]


def _conv_shift_matrix(lc):
    tap, t = jnp.meshgrid(jnp.arange(CONV_W), jnp.arange(lc), indexing="ij")
    src = (t - (CONV_W - 1) + tap).reshape(-1)
    col = jnp.arange(3 * PREV_ROWS + lc)[None, :]
    in_block = (src[:, None] >= 0) & (col == 3 * PREV_ROWS + src[:, None])
    in_prev = (src[:, None] < 0) & (col < 3 * PREV_ROWS) & (col % PREV_ROWS == PREV_ROWS + src[:, None])
    return (in_block | in_prev).astype(BF16)


def _split_prev_rows(conv_prev):
    b = conv_prev.shape[0]
    p = jnp.concatenate([jnp.zeros((b, PREV_ROWS - (CONV_W - 1), D_CONV), F32), conv_prev.astype(F32)], axis=1)
    hi = p.astype(BF16)
    r1 = p - hi.astype(F32)
    mid = r1.astype(BF16)
    lo = (r1 - mid.astype(F32)).astype(BF16)
    return jnp.concatenate([lo, mid, hi], axis=1)


def _ssd(xbc, dt, small, z, conv_prev, st0, conv_w, conv_b, alog_all, dskip, normw, tri, *, batch, seq_len):
    lc = min(SSD_CHUNK, seq_len)
    assert lc >= PREV_ROWS and seq_len % lc == 0
    nc = seq_len // lc
    shift = _conv_shift_matrix(lc)
    conv_prev = _split_prev_rows(conv_prev)
    blk = lambda b, c: (b * nc + c, 0)
    per_b3 = lambda b, c: (b, 0, 0)
    per_b4 = lambda b, c: (b, 0, 0, 0)
    m = batch * seq_len
    return pl.pallas_call(
        functools.partial(_ssd_kernel, lc=lc),
        grid=(batch, nc),
        in_specs=[
            pl.BlockSpec((lc, D_CONV), blk), pl.BlockSpec((lc, D_SSM), blk), pl.BlockSpec((lc, LANES), blk),
            pl.BlockSpec((lc, D_SSM), blk),
            pl.BlockSpec((1, 3 * PREV_ROWS, D_CONV), per_b3),
            pl.BlockSpec((1, SSD_GROUPS, D_STATE, GROUP_W), per_b4),
            _const_spec((CONV_W, D_CONV)), _const_spec((1, D_CONV)), _const_spec((1, D_SSM + LANES)),
            _const_spec((1, D_SSM)), _const_spec((1, D_SSM)), _const_spec((SSD_CHUNK, SSD_CHUNK)),
            _const_spec(shift.shape),
        ],
        out_specs=[
            pl.BlockSpec((lc, D_SSM), blk),
            pl.BlockSpec((1, SSD_GROUPS, D_STATE, GROUP_W), per_b4),
            pl.BlockSpec((1, SUBLANES, D_CONV), per_b3),
        ],
        out_shape=[
            jax.ShapeDtypeStruct((m, D_SSM), BF16),
            jax.ShapeDtypeStruct((batch, SSD_GROUPS, D_STATE, GROUP_W), F32),
            jax.ShapeDtypeStruct((batch, SUBLANES, D_CONV), F32),
        ],
        scratch_shapes=[
            pltpu.VMEM((3 * PREV_ROWS, D_CONV), BF16),
            pltpu.VMEM((SSD_GROUPS, D_STATE, GROUP_W), F32),
            pltpu.VMEM((SSD_CHUNK, D_SSM), F32),
        ],
        compiler_params=pltpu.CompilerParams(dimension_semantics=("arbitrary", "arbitrary"),
                                             vmem_limit_bytes=VMEM_LIMIT),
        name="ssd",
    )(xbc, dt, small, z, conv_prev, st0, conv_w, conv_b, alog_all, dskip, normw, tri, shift)


def _fox_prompt_kernel(q_ref, ka_ref, vt_ref, o_ref, qa_scr, sta_scr, stb_scr, acc_scr, *, t):
    qi = pl.program_id(2)
    lane = lax.broadcasted_iota(jnp.int32, (t, LANES), 1)
    q = q_ref[...]
    for j in range(2):
        own = (lane < FOX_HEAD_DIM) if j == 0 else (lane >= FOX_HEAD_DIM)
        pick = jnp.where(lane >= AUG_TERMS * j, jnp.where(lane < AUG_TERMS * (j + 1), 1.0, 0.0), 0.0)
        qa_scr[j] = jnp.concatenate([jnp.where(own, q, jnp.zeros_like(q)), pick.astype(BF16)], axis=1)
    acc_scr[...] = jnp.zeros(acc_scr.shape, F32)

    def logits(ki, st_scr):
        start = pl.multiple_of(ki * t, t)
        ka = ka_ref[pl.ds(start, t), :]
        for j in range(2):
            st_scr[j] = lax.dot_general(ka, qa_scr[j], (((1,), (1,)), ((), ())),
                                        preferred_element_type=F32)

    def consume(ki, st_scr, m_prev, masked):
        out = []
        for j in range(2):
            st = st_scr[j]
            if masked:
                rowi = lax.broadcasted_iota(jnp.int32, (t, t), 0)
                coli = lax.broadcasted_iota(jnp.int32, (t, t), 1)
                st = jnp.where(rowi <= coli, st, -jnp.inf)
            m_new = jnp.maximum(m_prev[j], jnp.max(st, axis=0, keepdims=True))
            alpha = jnp.exp2(m_prev[j] - m_new)
            pexp = jnp.exp2(st - m_new).astype(BF16)
            vt = vt_ref[ki, j * VT_ROWS:(j + 1) * VT_ROWS, :]
            acc_scr[j] = alpha * acc_scr[j] + jnp.dot(vt, pexp, preferred_element_type=F32)
            out.append(m_new)
        return tuple(out)

    logits(0, sta_scr)

    def body(i, m):
        logits(2 * i + 1, stb_scr)
        m = consume(2 * i, sta_scr, m, False)
        logits(2 * i + 2, sta_scr)
        return consume(2 * i + 1, stb_scr, m, False)

    m0 = tuple(jnp.full((1, t), -jnp.inf, F32) for _ in range(2))
    m = lax.fori_loop(0, qi // 2, body, m0)
    odd = lax.rem(qi, 2) == 1

    @pl.when(jnp.logical_not(odd))
    def _():
        consume(qi, sta_scr, m, True)

    @pl.when(odd)
    def _():
        logits(qi, stb_scr)
        m1 = consume(qi - 1, sta_scr, m, False)
        consume(qi, stb_scr, m1, True)

    halves = []
    for j in range(2):
        acc = acc_scr[j]
        halves.append(acc[:FOX_HEAD_DIM] * (1.0 / acc[FOX_HEAD_DIM:FOX_HEAD_DIM + 1]))
    o_ref[...] = jnp.concatenate(halves, axis=0).T.astype(BF16)


def _fox_prompt(q, kaug, vt, *, batch, seq_len, t):
    nq = seq_len // t
    m = batch * seq_len
    return pl.pallas_call(
        functools.partial(_fox_prompt_kernel, t=t),
        grid=(batch, FOX_HEADS // 2, nq),
        in_specs=[
            pl.BlockSpec((t, LANES), lambda b, p, i: (b * nq + i, p)),
            pl.BlockSpec((seq_len, KAUG_W), lambda b, p, i: (b, p)),
            pl.BlockSpec((nq, 2 * VT_ROWS, t), lambda b, p, i: (b, p, 0)),
        ],
        out_specs=pl.BlockSpec((t, LANES), lambda b, p, i: (b * nq + i, p)),
        out_shape=jax.ShapeDtypeStruct((m, D_FOX), BF16),
        scratch_shapes=[
            pltpu.VMEM((2, t, KAUG_W), BF16),
            pltpu.VMEM((2, t, t), F32),
            pltpu.VMEM((2, t, t), F32),
            pltpu.VMEM((2, VT_ROWS, t), F32),
        ],
        compiler_params=pltpu.CompilerParams(dimension_semantics=("arbitrary", "arbitrary", "arbitrary"),
                                             vmem_limit_bytes=VMEM_LIMIT),
        name="fox_prompt",
    )(q, kaug, vt)


def _fox_sample_kernel(q_ref, kc_ref, vc_ref, lfc_ref, kn_ref, vn_ref, lfn_ref, own_ref, qpos_ref, o_ref,
                       qbd_scr, m_scr, l_scr, acc_scr, carry_scr, *, tn):
    j = pl.program_id(1)
    nk = pl.num_programs(1)
    rows = FOX_HEADS * tn
    own = own_ref[...] > 0.0

    @pl.when(j == 0)
    def _():
        qt = jnp.concatenate([q_ref[...]] * FOX_HEADS, axis=0)
        qbd_scr[...] = jnp.where(own, qt, jnp.zeros_like(qt))
        m_scr[...] = jnp.full(m_scr.shape, -jnp.inf, F32)
        l_scr[...] = jnp.zeros(l_scr.shape, F32)
        acc_scr[...] = jnp.zeros(acc_scr.shape, F32)
        carry_scr[...] = jnp.zeros(carry_scr.shape, F32)

    def expand(ct):
        n = ct.shape[-1]
        return jnp.concatenate([jnp.broadcast_to(ct[h:h + 1], (tn, n)) for h in range(FOX_HEADS)], axis=0)

    def update(s, vals):
        m_prev = m_scr[...]
        m_new = jnp.maximum(m_prev, jnp.max(s, axis=-1, keepdims=True))
        alpha = jnp.exp(m_prev - m_new)
        pexp = jnp.exp(s - m_new)
        l_scr[...] = alpha * l_scr[...] + jnp.sum(pexp, axis=-1, keepdims=True)
        acc_scr[...] = alpha * acc_scr[...] + jnp.dot(pexp.astype(BF16), vals, preferred_element_type=F32)
        m_scr[...] = m_new

    nt_dims = (((1,), (1,)), ((), ()))
    ct = _lane_cumsum(lfc_ref[0]) + carry_scr[:, 0:1]
    carry_scr[...] = jnp.broadcast_to(ct[:, -1:], carry_scr.shape)
    tk = kc_ref.shape[1]
    kb = kc_ref[0].reshape(tk, D_FOX).astype(BF16)
    s = lax.dot_general(qbd_scr[...], kb, nt_dims, preferred_element_type=F32) - expand(ct)
    update(s, vc_ref[0].reshape(tk, D_FOX).astype(BF16))

    @pl.when(j == nk - 1)
    def _():
        cn = _lane_cumsum(lfn_ref[0]) + carry_scr[:, 0:1]
        kn = _pad_rows(kn_ref[...], LANES)
        vn = _pad_rows(vn_ref[...], LANES)
        sn = lax.dot_general(qbd_scr[...], kn, nt_dims, preferred_element_type=F32) - expand(cn)
        cidx = lax.broadcasted_iota(jnp.int32, (rows, LANES), 1)
        sn = jnp.where(cidx <= qpos_ref[...], sn, -jnp.inf)
        update(sn, vn)
        o = acc_scr[...] / l_scr[...]
        o = jnp.where(own, o, 0.0)
        out = o[0:tn]
        for h in range(1, FOX_HEADS):
            out = out + o[h * tn:(h + 1) * tn]
        o_ref[...] = out.astype(BF16)


def _fox_sample(q, kc, vc, lfc_t, kn, vn, lfn_t, *, batch, tn, past, tk):
    nk = past // tk
    rows = FOX_HEADS * tn
    r = jnp.arange(rows)
    own = ((r // tn)[:, None] == (jnp.arange(D_FOX) // FOX_HEAD_DIM)[None, :]).astype(F32)
    qpos = (r % tn).astype(jnp.int32).reshape(rows, 1)
    return pl.pallas_call(
        functools.partial(_fox_sample_kernel, tn=tn),
        grid=(batch, nk),
        in_specs=[
            pl.BlockSpec((tn, D_FOX), lambda b, j: (b, 0)),
            pl.BlockSpec((1, tk, FOX_HEADS, FOX_HEAD_DIM), lambda b, j: (b, j, 0, 0)),
            pl.BlockSpec((1, tk, FOX_HEADS, FOX_HEAD_DIM), lambda b, j: (b, j, 0, 0)),
            pl.BlockSpec((1, FOX_HEADS, tk), lambda b, j: (b, 0, j)),
            pl.BlockSpec((tn, D_FOX), lambda b, j: (b, 0)),
            pl.BlockSpec((tn, D_FOX), lambda b, j: (b, 0)),
            pl.BlockSpec((1, FOX_HEADS, LANES), lambda b, j: (b, 0, 0)),
            _const_spec((rows, D_FOX)), _const_spec((rows, 1)),
        ],
        out_specs=pl.BlockSpec((tn, D_FOX), lambda b, j: (b, 0)),
        out_shape=jax.ShapeDtypeStruct((batch * tn, D_FOX), BF16),
        scratch_shapes=[
            pltpu.VMEM((rows, D_FOX), BF16),
            pltpu.VMEM((rows, 1), F32),
            pltpu.VMEM((rows, 1), F32),
            pltpu.VMEM((rows, D_FOX), F32),
            pltpu.VMEM((FOX_HEADS, LANES), F32),
        ],
        compiler_params=pltpu.CompilerParams(dimension_semantics=("arbitrary", "arbitrary"),
                                             vmem_limit_bytes=VMEM_LIMIT),
        name="fox_sample",
    )(q, kc, vc, lfc_t, kn, vn, lfn_t, own, qpos)


def _out_mlp_kernel(x_ref, ys_ref, yf_ref, wos_ref, wof_ref, g_ref, wup_ref, wdn_ref, o_ref):
    x1 = (x_ref[...]
          + jnp.dot(ys_ref[...], wos_ref[...], preferred_element_type=F32)
          + jnp.dot(yf_ref[...], wof_ref[...], preferred_element_type=F32))
    ms = jnp.mean(x1 * x1, axis=-1, keepdims=True)
    h = (x1 * lax.rsqrt(ms + EPS) * g_ref[...]).astype(BF16)
    up = jnp.dot(h, wup_ref[...], preferred_element_type=F32)
    a = jnp.square(jnp.maximum(up, 0.0)).astype(BF16)
    o_ref[...] = x1 + jnp.dot(a, wdn_ref[...], preferred_element_type=F32)


def _out_mlp(x2d, ys, yf, wo_s, wo_f, g2, w_up, w_dn, *, tm):
    m, d = x2d.shape
    d_ff = w_up.shape[1]
    row = lambda i: (i, 0)
    return pl.pallas_call(
        _out_mlp_kernel,
        grid=(m // tm,),
        in_specs=[
            pl.BlockSpec((tm, d), row), pl.BlockSpec((tm, D_SSM), row), pl.BlockSpec((tm, D_FOX), row),
            _const_spec((D_SSM, d)), _const_spec((D_FOX, d)), _const_spec((1, d)),
            _const_spec((d, d_ff)), _const_spec((d_ff, d)),
        ],
        out_specs=pl.BlockSpec((tm, d), row),
        out_shape=jax.ShapeDtypeStruct((m, d), F32),
        compiler_params=pltpu.CompilerParams(dimension_semantics=("arbitrary",), vmem_limit_bytes=VMEM_LIMIT),
        name="out_mlp",
    )(x2d, ys, yf, wo_s, wo_f, g2, w_up, w_dn)


def _state_to_kernel_layout(st):
    b = st.shape[0]
    st = st.reshape(b, SSD_GROUPS, HEADS_PER_GROUP, SSD_HEAD_DIM, D_STATE)
    return st.transpose(0, 1, 4, 2, 3).reshape(b, SSD_GROUPS, D_STATE, GROUP_W)


def _state_from_kernel_layout(st):
    b = st.shape[0]
    st = st.reshape(b, SSD_GROUPS, D_STATE, HEADS_PER_GROUP, SSD_HEAD_DIM)
    return st.transpose(0, 1, 3, 4, 2).reshape(b, SSD_HEADS, SSD_HEAD_DIM, D_STATE)


def _layer_weights(norm1_w, w_in, conv_w, conv_b, dt_bias, A_log, D_skip, ssd_norm_w, f_bias,
                   q_norm_w, k_norm_w, w_out, norm2_w, w_up, w_down):
    d = w_in.shape[0]
    splits = [D_SSM, D_CONV, SSD_HEADS, D_FOX, D_FOX, D_FOX]
    idx = [sum(splits[:i + 1]) for i in range(len(splits))]
    w_z, w_xbc, w_dt, w_q, w_k, w_v, w_f = jnp.split(w_in, idx, axis=1)
    pad = jnp.zeros((d, LANES - SSD_HEADS - FOX_HEADS), w_in.dtype)
    w_all = jnp.concatenate(
        [w_z, w_xbc, jnp.repeat(w_dt, SSD_HEAD_DIM, axis=1), w_q, w_k, w_v, w_dt, w_f, pad], axis=1).astype(BF16)
    bias_all = jnp.concatenate(
        [jnp.repeat(dt_bias, SSD_HEAD_DIM), dt_bias, f_bias,
         jnp.zeros((LANES - SSD_HEADS - FOX_HEADS,), F32)]).reshape(1, D_SSM + LANES).astype(F32)
    head_id = jnp.arange(D_FOX) // FOX_HEAD_DIM
    bd = (head_id[:, None] == head_id[None, :]).astype(BF16)
    alog_all = jnp.concatenate(
        [jnp.repeat(A_log, SSD_HEAD_DIM), A_log, jnp.zeros((LANES - SSD_HEADS,), F32)]).reshape(1, D_SSM + LANES)
    r = jnp.arange(SSD_CHUNK)
    tri = (r[None, :] <= r[:, None]).astype(BF16)
    hh = jnp.arange(FOX_HEADS)
    place = jnp.zeros((AUG_TERMS, LANES, D_FOX), F32)
    for t in range(AUG_TERMS):
        place = place.at[t, SMALL_F + hh, (hh // 2) * LANES + (hh % 2) * AUG_TERMS + t].set(1.0)
    place = place.astype(BF16)
    return dict(
        g1=norm1_w.reshape(1, d).astype(F32), w_all=w_all, bias_all=bias_all, bd=bd,
        qw=jnp.tile(q_norm_w, FOX_HEADS).reshape(1, D_FOX).astype(F32),
        kw=jnp.tile(k_norm_w, FOX_HEADS).reshape(1, D_FOX).astype(F32),
        conv_w=conv_w.astype(F32), conv_b=conv_b.reshape(1, D_CONV).astype(F32), alog_all=alog_all.astype(F32),
        dskip=jnp.repeat(D_skip, SSD_HEAD_DIM).reshape(1, D_SSM).astype(F32),
        normw=ssd_norm_w.reshape(1, D_SSM).astype(F32), tri=tri, place=place,
        wo_s=w_out[:D_SSM].astype(BF16), wo_f=w_out[D_SSM:].astype(BF16),
        g2=norm2_w.reshape(1, d).astype(F32), w_up=w_up.astype(BF16), w_dn=w_down.astype(BF16),
    )


def _pick_tile(n, pref):
    t = min(pref, n)
    while n % t:
        t //= 2
    return t


def _trunk(x, conv_prev, ssm_prev, w, fox_fn, *, prompt):
    b, l, d = x.shape
    m = b * l
    x2d = x.reshape(m, d)
    tm = _pick_tile(l if prompt else m, 512)
    z, xbc, dt, q, kf, vf, small, ka, va = _in_proj(
        x2d, w["g1"], w["w_all"], w["bias_all"], w["bd"], w["qw"], w["kw"], w["place"],
        tm=tm, seq_len=l, prompt=prompt)
    y_ssd, st_out, conv_out = _ssd(xbc, dt, small, z, conv_prev, _state_to_kernel_layout(ssm_prev.astype(F32)),
                                   w["conv_w"], w["conv_b"], w["alog_all"], w["dskip"], w["normw"], w["tri"],
                                   batch=b, seq_len=l)
    y_fox = fox_fn(q, ka, va, small, tm)
    y = _out_mlp(x2d, y_ssd, y_fox, w["wo_s"], w["wo_f"], w["g2"], w["w_up"], w["w_dn"], tm=_pick_tile(m, 512))
    logf = small[:, SMALL_F:SMALL_F + FOX_HEADS].reshape(b, l, FOX_HEADS)
    states = (kf.reshape(b, l, FOX_HEADS, FOX_HEAD_DIM), vf.reshape(b, l, FOX_HEADS, FOX_HEAD_DIM), logf,
              _state_from_kernel_layout(st_out), conv_out[:, SUBLANES - (CONV_W - 1):, :])
    return y.reshape(b, l, d), states


def kernel(x_prompt, x_sample, cache_k, cache_v, cache_logf, state_ssm, state_conv, norm1_w, w_in, conv_w, conv_b, dt_bias, A_log, D_skip, ssd_norm_w, f_bias, q_norm_w, k_norm_w, w_out, norm2_w, w_up, w_down):
    depth = w_in.shape[0]
    b_p, l_p, _ = x_prompt.shape
    b_s, l_s, _ = x_sample.shape
    past = cache_k.shape[2]
    y_prompt, y_sample = x_prompt, x_sample
    p_states, s_states = [], []
    for layer in range(depth):
        w = _layer_weights(norm1_w[layer], w_in[layer], conv_w[layer], conv_b[layer], dt_bias[layer], A_log[layer],
                           D_skip[layer], ssd_norm_w[layer], f_bias[layer], q_norm_w[layer], k_norm_w[layer],
                           w_out[layer], norm2_w[layer], w_up[layer], w_down[layer])

        def fox_prompt_fn(q, kaug, vt, small, tm):
            return _fox_prompt(q, kaug, vt, batch=b_p, seq_len=l_p, t=tm)

        def fox_sample_fn(q, kb, vb, small, tm, layer=layer):
            lf_new = small[:, SMALL_F:SMALL_F + FOX_HEADS].reshape(b_s, l_s, FOX_HEADS).transpose(0, 2, 1)
            lf_new = jnp.pad(lf_new, ((0, 0), (0, 0), (0, LANES - l_s)))
            lf_cache = cache_logf[layer].astype(F32).transpose(0, 2, 1)
            return _fox_sample(q, cache_k[layer], cache_v[layer],
                               lf_cache, kb, vb, lf_new, batch=b_s, tn=l_s, past=past, tk=_pick_tile(past, 1024))

        conv0 = jnp.zeros((b_p, CONV_W - 1, D_CONV), F32)
        ssm0 = jnp.zeros((b_p, SSD_HEADS, SSD_HEAD_DIM, D_STATE), F32)
        y_prompt, st_p = _trunk(y_prompt, conv0, ssm0, w, fox_prompt_fn, prompt=True)
        y_sample, st_s = _trunk(y_sample, state_conv[layer], state_ssm[layer], w, fox_sample_fn, prompt=False)
        p_states.append(st_p)
        s_states.append(st_s)
    stack = lambda states, i: jnp.stack([s[i] for s in states])
    return (y_prompt, y_sample,
            stack(p_states, 0), stack(p_states, 1), stack(p_states, 2), stack(p_states, 3), stack(p_states, 4),
            stack(s_states, 0), stack(s_states, 1), stack(s_states, 2), stack(s_states, 3), stack(s_states, 4))
```

```python
import functools
import math

import jax
import jax.numpy as jnp
from jax import lax
from jax.experimental import pallas as pl
from jax.experimental.pallas import tpu as pltpu

F32 = jnp.float32
BF16 = jnp.bfloat16

EPS = 1e-6
SSD_HEADS = 16
SSD_HEAD_DIM = 64
SSD_GROUPS = 2
HEADS_PER_GROUP = SSD_HEADS // SSD_GROUPS
D_STATE = 128
CONV_W = 4
D_SSM = SSD_HEADS * SSD_HEAD_DIM
D_BC = SSD_GROUPS * D_STATE
D_CONV = D_SSM + 2 * D_BC
GROUP_W = D_SSM // SSD_GROUPS
FOX_HEADS = 8
FOX_HEAD_DIM = 64
D_FOX = FOX_HEADS * FOX_HEAD_DIM

LANES = 128
SUBLANES = 8
SSD_CHUNK = 128
PREV_ROWS = 16
SSD_STREAMS = 1
VMEM_LIMIT = 56 * 1024 * 1024

C_Z = 0
C_XBC = C_Z + D_SSM
C_DT = C_XBC + D_CONV
C_Q = C_DT + D_SSM
C_K = C_Q + D_FOX
C_V = C_K + D_FOX
C_SMALL = C_V + D_FOX
N_PROJ = C_SMALL + LANES
SMALL_DT = 0
SMALL_F = SSD_HEADS
KAUG_W = 2 * LANES
AUG_TERMS = 3
LOG2E = math.log2(math.e)
Q_SCALE = FOX_HEAD_DIM ** -0.5 * LOG2E
VT_ROWS = 2 * FOX_HEAD_DIM


def _softplus(x):
    return jnp.maximum(x, 0.0) + jnp.log1p(jnp.exp(-jnp.abs(x)))


def _silu(x):
    return x * (1.0 / (1.0 + jnp.exp(-x)))


def _lane_cumsum(x):
    n = x.shape[-1]
    lane = lax.broadcasted_iota(jnp.int32, x.shape, x.ndim - 1)
    k = 1
    while k < n:
        x = x + jnp.where(lane >= k, pltpu.roll(x, k, axis=x.ndim - 1), 0.0)
        k *= 2
    return x


def _const_spec(shape):
    zeros = (0,) * len(shape)
    return pl.BlockSpec(shape, lambda *_: zeros, pipeline_mode=pl.Buffered(1))


def _in_proj_kernel(x_ref, g_ref, w_ref, bias_ref, bd_ref, qw_ref, kw_ref, *rest, tiles_per_seq, prompt):
    if prompt:
        (place_ref, z_ref, xbc_ref, dt_ref, q_ref, kf_ref, vf_ref, small_ref, kaug_ref, vt_ref, carry_ref) = rest
    else:
        (z_ref, xbc_ref, dt_ref, q_ref, kf_ref, vf_ref, small_ref, kb_ref, vb_ref) = rest
    x = x_ref[...]
    ms = jnp.mean(x * x, axis=-1, keepdims=True)
    h = (x * lax.rsqrt(ms + EPS) * g_ref[...]).astype(BF16)

    def proj(lo, hi):
        return jnp.dot(h, w_ref[:, lo:hi], preferred_element_type=F32)

    z_ref[...] = proj(C_Z, C_XBC).astype(BF16)
    sm = proj(C_SMALL, N_PROJ) + bias_ref[:, D_SSM:D_SSM + LANES]
    lane = lax.broadcasted_iota(jnp.int32, sm.shape, 1)
    sm = jnp.where(lane < SMALL_F, _softplus(sm), -_softplus(-sm))
    small_ref[...] = sm
    if prompt:
        @pl.when(pl.program_id(0) % tiles_per_seq == 0)
        def _():
            carry_ref[...] = jnp.zeros_like(carry_ref)

        cs_t = _lane_cumsum(sm.T) + carry_ref[:, 0:1]
        carry_ref[...] = jnp.broadcast_to(cs_t[:, -1:], carry_ref.shape)
        negc = cs_t.T * (-LOG2E)
        hi = negc.astype(BF16)
        r1 = negc - hi.astype(F32)
        mid = r1.astype(BF16)
        lo = (r1 - mid.astype(F32)).astype(BF16)

    xbc_ref[...] = proj(C_XBC, C_DT).astype(BF16)
    dt_ref[...] = _softplus(proj(C_DT, C_Q) + bias_ref[:, 0:D_SSM])

    def head_rms(t, w):
        ss = jnp.dot((t * t).astype(BF16), bd_ref[...], preferred_element_type=F32)
        return t * lax.rsqrt(ss * (1.0 / FOX_HEAD_DIM) + EPS) * w

    qn = head_rms(proj(C_Q, C_K), qw_ref[...])
    q_ref[...] = (qn * Q_SCALE).astype(BF16) if prompt else (qn * (FOX_HEAD_DIM ** -0.5)).astype(BF16)
    kn = head_rms(proj(C_K, C_V), kw_ref[...])
    v = proj(C_V, C_SMALL)
    kf_ref[...] = kn.reshape(kf_ref.shape)
    vf_ref[...] = v.reshape(vf_ref.shape)

    kb = kn.astype(BF16)
    if not prompt:
        kb_ref[...] = kb
        vb_ref[...] = v.astype(BF16)
        return

    for p in range(FOX_HEADS // 2):
        kaug_ref[:, p * KAUG_W:p * KAUG_W + LANES] = kb[:, p * LANES:(p + 1) * LANES]
    v_t = v.T
    tm = v_t.shape[1]
    sub = lax.broadcasted_iota(jnp.int32, (VT_ROWS - FOX_HEAD_DIM, tm), 0)
    ones_then_zeros = jnp.where(sub == 0, 1.0, 0.0)
    for hd in range(FOX_HEADS):
        blk = jnp.concatenate([v_t[hd * FOX_HEAD_DIM:(hd + 1) * FOX_HEAD_DIM], ones_then_zeros], axis=0)
        vt_ref[0, hd * VT_ROWS:(hd + 1) * VT_ROWS, :] = blk.astype(BF16)
    aug = (jnp.dot(hi, place_ref[0], preferred_element_type=F32)
           + jnp.dot(mid, place_ref[1], preferred_element_type=F32)
           + jnp.dot(lo, place_ref[2], preferred_element_type=F32)).astype(BF16)
    for p in range(FOX_HEADS // 2):
        kaug_ref[:, p * KAUG_W + LANES:(p + 1) * KAUG_W] = aug[:, p * LANES:(p + 1) * LANES]


def _in_proj(x2d, g, w_all, bias_all, bd, qw, kw, place, *, tm, seq_len, prompt):
    m, d = x2d.shape
    nt = m // tm
    row = lambda i: (i, 0)
    out_shape = [
        jax.ShapeDtypeStruct((m, D_SSM), BF16),
        jax.ShapeDtypeStruct((m, D_CONV), BF16),
        jax.ShapeDtypeStruct((m, D_SSM), F32),
        jax.ShapeDtypeStruct((m, D_FOX), BF16),
        jax.ShapeDtypeStruct((m, FOX_HEADS, FOX_HEAD_DIM), F32),
        jax.ShapeDtypeStruct((m, FOX_HEADS, FOX_HEAD_DIM), F32),
        jax.ShapeDtypeStruct((m, LANES), F32),
    ]
    row3 = lambda i: (i, 0, 0)
    out_specs = [
        pl.BlockSpec((tm, D_SSM), row), pl.BlockSpec((tm, D_CONV), row), pl.BlockSpec((tm, D_SSM), row),
        pl.BlockSpec((tm, D_FOX), row),
        pl.BlockSpec((tm, FOX_HEADS, FOX_HEAD_DIM), row3), pl.BlockSpec((tm, FOX_HEADS, FOX_HEAD_DIM), row3),
        pl.BlockSpec((tm, LANES), row),
    ]
    in_specs = [
        pl.BlockSpec((tm, d), row),
        _const_spec((1, d)), _const_spec((d, N_PROJ)), _const_spec((1, D_SSM + LANES)),
        _const_spec((D_FOX, D_FOX)), _const_spec((1, D_FOX)), _const_spec((1, D_FOX)),
    ]
    args = [x2d, g, w_all, bias_all, bd, qw, kw]
    scratch = []
    if prompt:
        in_specs.append(_const_spec(place.shape))
        args.append(place)
        out_shape += [jax.ShapeDtypeStruct((m, (FOX_HEADS // 2) * KAUG_W), BF16),
                      jax.ShapeDtypeStruct((nt, FOX_HEADS * VT_ROWS, tm), BF16)]
        out_specs += [pl.BlockSpec((tm, (FOX_HEADS // 2) * KAUG_W), row),
                      pl.BlockSpec((1, FOX_HEADS * VT_ROWS, tm), lambda i: (i, 0, 0))]
        scratch.append(pltpu.VMEM((LANES, LANES), F32))
    else:
        out_shape += [jax.ShapeDtypeStruct((m, D_FOX), BF16), jax.ShapeDtypeStruct((m, D_FOX), BF16)]
        out_specs += [pl.BlockSpec((tm, D_FOX), row), pl.BlockSpec((tm, D_FOX), row)]
    kern = functools.partial(_in_proj_kernel, tiles_per_seq=max(seq_len // tm, 1), prompt=prompt)
    return pl.pallas_call(
        kern,
        grid=(nt,),
        in_specs=in_specs,
        out_specs=out_specs,
        out_shape=out_shape,
        scratch_shapes=scratch,
        compiler_params=pltpu.CompilerParams(dimension_semantics=("arbitrary",), vmem_limit_bytes=VMEM_LIMIT),
        name="in_proj",
    )(*args)


def _pad_rows(x, rows):
    if x.shape[0] == rows:
        return x
    return jnp.concatenate([x, jnp.zeros((rows - x.shape[0],) + x.shape[1:], x.dtype)], axis=0)


def _ssd_kernel(xbc_ref, dt_ref, small_ref, z_ref, convprev_ref, st0_ref,
                convw_ref, convb_ref, alog_ref, dskip_ref, normw_ref, tri_ref, shift_ref,
                y_ref, stout_ref, convout_ref,
                prev_scr, st_scr, ydiag_scr, *, lc, streams):
    for s in range(streams):
        _ssd_stream(s, xbc_ref, dt_ref, small_ref, z_ref, convprev_ref, st0_ref,
                    convw_ref, convb_ref, alog_ref, dskip_ref, normw_ref, tri_ref, shift_ref,
                    y_ref, stout_ref, convout_ref, prev_scr, st_scr, ydiag_scr, lc)


def _ssd_stream(s, xbc_ref, dt_ref, small_ref, z_ref, convprev_ref, st0_ref,
                convw_ref, convb_ref, alog_ref, dskip_ref, normw_ref, tri_ref, shift_ref,
                y_ref, stout_ref, convout_ref, prev_scr, st_scr, ydiag_scr, lc):
    c = pl.program_id(1)
    last_chunk = c == pl.num_programs(1) - 1
    P = SSD_CHUNK

    @pl.when(c == 0)
    def _():
        prev_scr[s] = convprev_ref[s]
        st_scr[s] = st0_ref[s]

    cur = xbc_ref[s]
    shifted = jnp.dot(shift_ref[...], jnp.concatenate([prev_scr[s], cur], axis=0),
                      preferred_element_type=F32)
    conv = convb_ref[...]
    for k in range(CONV_W):
        conv = conv + convw_ref[k:k + 1, :] * shifted[k * lc:(k + 1) * lc]
    tail = cur[lc - PREV_ROWS:lc]
    prev_scr[s, 0:2 * PREV_ROWS] = jnp.zeros((2 * PREV_ROWS, D_CONV), BF16)
    prev_scr[s, 2 * PREV_ROWS:3 * PREV_ROWS] = tail

    @pl.when(last_chunk)
    def _():
        convout_ref[s] = tail.astype(F32)[PREV_ROWS - SUBLANES:]

    act = _pad_rows(_silu(conv), P)
    dt = _pad_rows(dt_ref[s], P)
    dt_c = _pad_rows(small_ref[s], P)
    xs = act[:, :D_SSM]
    xdt = xs * dt

    a_all = -jnp.exp(alog_ref[...])
    da = jnp.concatenate([dt, dt_c], axis=1) * a_all
    da_hi = da.astype(BF16)
    da_lo = (da - da_hi.astype(F32)).astype(BF16)
    tri = tri_ref[...]
    acs_all = (jnp.dot(tri, da_hi, preferred_element_type=F32)
               + jnp.dot(tri, da_lo, preferred_element_type=F32))
    acs = acs_all[:, :D_SSM]
    acs_t = acs_all[:, D_SSM:].T
    a_last = acs[P - 1:P, :]
    ea = jnp.exp(acs)
    xt = (xdt * jnp.exp(a_last - acs)).astype(BF16)
    sd = jnp.exp(a_last)

    rowi = lax.broadcasted_iota(jnp.int32, (P, P), 0)
    coli = lax.broadcasted_iota(jnp.int32, (P, P), 1)
    tril = coli <= rowi
    lane = lax.broadcasted_iota(jnp.int32, (P, LANES), 1)
    first_half = lane < SSD_HEAD_DIM

    yoff = []
    for g in range(SSD_GROUPS):
        bg = act[:, D_SSM + g * D_STATE:D_SSM + (g + 1) * D_STATE]
        cg = act[:, D_SSM + D_BC + g * D_STATE:D_SSM + D_BC + (g + 1) * D_STATE]
        bgb = bg.astype(BF16)
        cgb = cg.astype(BF16)
        cb = lax.dot_general(cgb, bgb, (((1,), (1,)), ((), ())), preferred_element_type=F32)
        for pr in range(HEADS_PER_GROUP // 2):
            lo = g * GROUP_W + pr * LANES
            h0 = g * HEADS_PER_GROUP + 2 * pr
            slab = acs[:, lo:lo + LANES]
            rolled = pltpu.roll(slab, SSD_HEAD_DIM, axis=1)
            xpair = xdt[:, lo:lo + LANES]
            yhead = []
            for j in range(2):
                col = jnp.where(first_half, slab, rolled) if j == 0 else jnp.where(first_half, rolled, slab)
                seg = col - acs_t[h0 + j:h0 + j + 1, :]
                dec = jnp.exp(jnp.where(tril, seg, -jnp.inf))
                mh = (cb * dec).astype(BF16)
                keep = first_half if j == 0 else jnp.logical_not(first_half)
                xm = jnp.where(keep, xpair, 0.0).astype(BF16)
                yhead.append(jnp.dot(mh, xm, preferred_element_type=F32))
            ydiag_scr[s, :, lo:lo + LANES] = yhead[0] + yhead[1]
        gs = slice(g * GROUP_W, (g + 1) * GROUP_W)
        st = st_scr[s, g]
        yoff.append(jnp.dot(cgb, st.astype(BF16), preferred_element_type=F32) * ea[:, gs])
        st_scr[s, g] = st * sd[:, gs] + jnp.dot(bg.T.astype(BF16), xt[:, gs], preferred_element_type=F32)

    y = ydiag_scr[s] + jnp.concatenate(yoff, axis=1) + xs * dskip_ref[...]
    y = y[:lc] * _silu(z_ref[s].astype(F32))
    outs = []
    for g in range(SSD_GROUPS):
        yg = y[:, g * GROUP_W:(g + 1) * GROUP_W]
        outs.append(yg * lax.rsqrt(jnp.mean(yg * yg, axis=-1, keepdims=True) + EPS))
    y_ref[s] = (jnp.concatenate(outs, axis=1) * normw_ref[...]).astype(BF16)

    @pl.when(last_chunk)
    def _():
        stout_ref[s] = st_scr[s]


def _conv_shift_matrix(lc):
    tap, t = jnp.meshgrid(jnp.arange(CONV_W), jnp.arange(lc), indexing="ij")
    src = (t - (CONV_W - 1) + tap).reshape(-1)
    col = jnp.arange(3 * PREV_ROWS + lc)[None, :]
    in_block = (src[:, None] >= 0) & (col == 3 * PREV_ROWS + src[:, None])
    in_prev = (src[:, None] < 0) & (col < 3 * PREV_ROWS) & (col % PREV_ROWS == PREV_ROWS + src[:, None])
    return (in_block | in_prev).astype(BF16)


def _split_prev_rows(conv_prev):
    b = conv_prev.shape[0]
    p = jnp.concatenate([jnp.zeros((b, PREV_ROWS - (CONV_W - 1), D_CONV), F32), conv_prev.astype(F32)], axis=1)
    hi = p.astype(BF16)
    r1 = p - hi.astype(F32)
    mid = r1.astype(BF16)
    lo = (r1 - mid.astype(F32)).astype(BF16)
    return jnp.concatenate([lo, mid, hi], axis=1)


def _ssd(xbc, dt, small, z, conv_prev, st0, conv_w, conv_b, alog_all, dskip, normw, tri, *, batch, seq_len):
    lc = min(SSD_CHUNK, seq_len)
    assert lc >= PREV_ROWS and seq_len % lc == 0
    nc = seq_len // lc
    shift = _conv_shift_matrix(lc)
    conv_prev = _split_prev_rows(conv_prev)
    ns = SSD_STREAMS if batch % SSD_STREAMS == 0 else 1
    blk = lambda b, c: (b, c, 0)
    per_b3 = lambda b, c: (b, 0, 0)
    per_b4 = lambda b, c: (b, 0, 0, 0)
    per_stream = lambda a: a.reshape(batch, seq_len, a.shape[-1])
    y, st_out, conv_out = pl.pallas_call(
        functools.partial(_ssd_kernel, lc=lc, streams=ns),
        grid=(batch // ns, nc),
        in_specs=[
            pl.BlockSpec((ns, lc, D_CONV), blk), pl.BlockSpec((ns, lc, D_SSM), blk),
            pl.BlockSpec((ns, lc, LANES), blk), pl.BlockSpec((ns, lc, D_SSM), blk),
            pl.BlockSpec((ns, 3 * PREV_ROWS, D_CONV), per_b3),
            pl.BlockSpec((ns, SSD_GROUPS, D_STATE, GROUP_W), per_b4),
            _const_spec((CONV_W, D_CONV)), _const_spec((1, D_CONV)), _const_spec((1, D_SSM + LANES)),
            _const_spec((1, D_SSM)), _const_spec((1, D_SSM)), _const_spec((SSD_CHUNK, SSD_CHUNK)),
            _const_spec(shift.shape),
        ],
        out_specs=[
            pl.BlockSpec((ns, lc, D_SSM), blk),
            pl.BlockSpec((ns, SSD_GROUPS, D_STATE, GROUP_W), per_b4),
            pl.BlockSpec((ns, SUBLANES, D_CONV), per_b3),
        ],
        out_shape=[
            jax.ShapeDtypeStruct((batch, seq_len, D_SSM), BF16),
            jax.ShapeDtypeStruct((batch, SSD_GROUPS, D_STATE, GROUP_W), F32),
            jax.ShapeDtypeStruct((batch, SUBLANES, D_CONV), F32),
        ],
        scratch_shapes=[
            pltpu.VMEM((ns, 3 * PREV_ROWS, D_CONV), BF16),
            pltpu.VMEM((ns, SSD_GROUPS, D_STATE, GROUP_W), F32),
            pltpu.VMEM((ns, SSD_CHUNK, D_SSM), F32),
        ],
        compiler_params=pltpu.CompilerParams(dimension_semantics=("arbitrary", "arbitrary"),
                                             vmem_limit_bytes=VMEM_LIMIT),
        name="ssd",
    )(per_stream(xbc), per_stream(dt), per_stream(small), per_stream(z), conv_prev, st0,
      conv_w, conv_b, alog_all, dskip, normw, tri, shift)
    return y.reshape(batch * seq_len, D_SSM), st_out, conv_out


def _fox_prompt_kernel(q_ref, ka_ref, vt_ref, o_ref, qa_scr, sta_scr, stb_scr, acc_scr, *, t):
    qi = pl.program_id(2)
    lane = lax.broadcasted_iota(jnp.int32, (t, LANES), 1)
    q = q_ref[...]
    for j in range(2):
        own = (lane < FOX_HEAD_DIM) if j == 0 else (lane >= FOX_HEAD_DIM)
        pick = jnp.where(lane >= AUG_TERMS * j, jnp.where(lane < AUG_TERMS * (j + 1), 1.0, 0.0), 0.0)
        qa_scr[j] = jnp.concatenate([jnp.where(own, q, jnp.zeros_like(q)), pick.astype(BF16)], axis=1)
    acc_scr[...] = jnp.zeros(acc_scr.shape, F32)

    def logits(ki, st_scr):
        start = pl.multiple_of(ki * t, t)
        ka = ka_ref[pl.ds(start, t), :]
        for j in range(2):
            st_scr[j] = lax.dot_general(ka, qa_scr[j], (((1,), (1,)), ((), ())),
                                        preferred_element_type=F32)

    def consume(ki, st_scr, m_prev, masked):
        out = []
        for j in range(2):
            st = st_scr[j]
            if masked:
                rowi = lax.broadcasted_iota(jnp.int32, (t, t), 0)
                coli = lax.broadcasted_iota(jnp.int32, (t, t), 1)
                st = jnp.where(rowi <= coli, st, -jnp.inf)
            m_new = jnp.maximum(m_prev[j], jnp.max(st, axis=0, keepdims=True))
            alpha = jnp.exp2(m_prev[j] - m_new)
            pexp = jnp.exp2(st - m_new).astype(BF16)
            vt = vt_ref[ki, j * VT_ROWS:(j + 1) * VT_ROWS, :]
            acc_scr[j] = alpha * acc_scr[j] + jnp.dot(vt, pexp, preferred_element_type=F32)
            out.append(m_new)
        return tuple(out)

    logits(0, sta_scr)

    def body(i, m):
        logits(2 * i + 1, stb_scr)
        m = consume(2 * i, sta_scr, m, False)
        logits(2 * i + 2, sta_scr)
        return consume(2 * i + 1, stb_scr, m, False)

    m0 = tuple(jnp.full((1, t), -jnp.inf, F32) for _ in range(2))
    m = lax.fori_loop(0, qi // 2, body, m0)
    odd = lax.rem(qi, 2) == 1

    @pl.when(jnp.logical_not(odd))
    def _():
        consume(qi, sta_scr, m, True)

    @pl.when(odd)
    def _():
        logits(qi, stb_scr)
        m1 = consume(qi - 1, sta_scr, m, False)
        consume(qi, stb_scr, m1, True)

    halves = []
    for j in range(2):
        acc = acc_scr[j]
        halves.append(acc[:FOX_HEAD_DIM] * (1.0 / acc[FOX_HEAD_DIM:FOX_HEAD_DIM + 1]))
    o_ref[...] = jnp.concatenate(halves, axis=0).T.astype(BF16)


def _fox_prompt(q, kaug, vt, *, batch, seq_len, t):
    nq = seq_len // t
    m = batch * seq_len
    return pl.pallas_call(
        functools.partial(_fox_prompt_kernel, t=t),
        grid=(batch, FOX_HEADS // 2, nq),
        in_specs=[
            pl.BlockSpec((t, LANES), lambda b, p, i: (b * nq + i, p)),
            pl.BlockSpec((seq_len, KAUG_W), lambda b, p, i: (b, p)),
            pl.BlockSpec((nq, 2 * VT_ROWS, t), lambda b, p, i: (b, p, 0)),
        ],
        out_specs=pl.BlockSpec((t, LANES), lambda b, p, i: (b * nq + i, p)),
        out_shape=jax.ShapeDtypeStruct((m, D_FOX), BF16),
        scratch_shapes=[
            pltpu.VMEM((2, t, KAUG_W), BF16),
            pltpu.VMEM((2, t, t), F32),
            pltpu.VMEM((2, t, t), F32),
            pltpu.VMEM((2, VT_ROWS, t), F32),
        ],
        compiler_params=pltpu.CompilerParams(dimension_semantics=("arbitrary", "arbitrary", "arbitrary"),
                                             vmem_limit_bytes=VMEM_LIMIT),
        name="fox_prompt",
    )(q, kaug, vt)


def _fox_sample_kernel(q_ref, kc_ref, vc_ref, lfc_ref, kn_ref, vn_ref, lfn_ref, own_ref, qpos_ref, o_ref,
                       qbd_scr, m_scr, l_scr, acc_scr, carry_scr, *, tn):
    j = pl.program_id(1)
    nk = pl.num_programs(1)
    rows = FOX_HEADS * tn
    own = own_ref[...] > 0.0

    @pl.when(j == 0)
    def _():
        qt = jnp.concatenate([q_ref[...]] * FOX_HEADS, axis=0)
        qbd_scr[...] = jnp.where(own, qt, jnp.zeros_like(qt))
        m_scr[...] = jnp.full(m_scr.shape, -jnp.inf, F32)
        l_scr[...] = jnp.zeros(l_scr.shape, F32)
        acc_scr[...] = jnp.zeros(acc_scr.shape, F32)
        carry_scr[...] = jnp.zeros(carry_scr.shape, F32)

    def expand(ct):
        n = ct.shape[-1]
        return jnp.concatenate([jnp.broadcast_to(ct[h:h + 1], (tn, n)) for h in range(FOX_HEADS)], axis=0)

    nt_dims = (((1,), (1,)), ((), ()))

    def update(s, vals, vals_transposed):
        m_prev = m_scr[...]
        m_new = jnp.maximum(m_prev, jnp.max(s, axis=-1, keepdims=True))
        alpha = jnp.exp(m_prev - m_new)
        pexp = jnp.exp(s - m_new)
        l_scr[...] = alpha * l_scr[...] + jnp.sum(pexp, axis=-1, keepdims=True)
        if vals_transposed:
            pv = lax.dot_general(pexp.astype(BF16), vals, nt_dims, preferred_element_type=F32)
        else:
            pv = jnp.dot(pexp.astype(BF16), vals, preferred_element_type=F32)
        acc_scr[...] = alpha * acc_scr[...] + pv
        m_scr[...] = m_new

    ct = _lane_cumsum(lfc_ref[0]) + carry_scr[:, 0:1]
    carry_scr[...] = jnp.broadcast_to(ct[:, -1:], carry_scr.shape)
    tk = kc_ref.shape[-1]
    kt = kc_ref[0].reshape(D_FOX, tk).astype(BF16)
    s = jnp.dot(qbd_scr[...], kt, preferred_element_type=F32) - expand(ct)
    update(s, vc_ref[0].reshape(D_FOX, tk).astype(BF16), True)

    @pl.when(j == nk - 1)
    def _():
        cn = _lane_cumsum(lfn_ref[0]) + carry_scr[:, 0:1]
        kn = _pad_rows(kn_ref[...], LANES)
        vn = _pad_rows(vn_ref[...], LANES)
        sn = lax.dot_general(qbd_scr[...], kn, nt_dims, preferred_element_type=F32) - expand(cn)
        cidx = lax.broadcasted_iota(jnp.int32, (rows, LANES), 1)
        sn = jnp.where(cidx <= qpos_ref[...], sn, -jnp.inf)
        update(sn, vn, False)
        o = acc_scr[...] / l_scr[...]
        o = jnp.where(own, o, 0.0)
        out = o[0:tn]
        for h in range(1, FOX_HEADS):
            out = out + o[h * tn:(h + 1) * tn]
        o_ref[...] = out.astype(BF16)


def _fox_sample(q, kc, vc, lfc_t, kn, vn, lfn_t, *, batch, tn, past, tk):
    nk = past // tk
    rows = FOX_HEADS * tn
    r = jnp.arange(rows)
    own = ((r // tn)[:, None] == (jnp.arange(D_FOX) // FOX_HEAD_DIM)[None, :]).astype(F32)
    qpos = (r % tn).astype(jnp.int32).reshape(rows, 1)
    return pl.pallas_call(
        functools.partial(_fox_sample_kernel, tn=tn),
        grid=(batch, nk),
        in_specs=[
            pl.BlockSpec((tn, D_FOX), lambda b, j: (b, 0)),
            pl.BlockSpec((1, FOX_HEADS, FOX_HEAD_DIM, tk), lambda b, j: (b, 0, 0, j)),
            pl.BlockSpec((1, FOX_HEADS, FOX_HEAD_DIM, tk), lambda b, j: (b, 0, 0, j)),
            pl.BlockSpec((1, FOX_HEADS, tk), lambda b, j: (b, 0, j)),
            pl.BlockSpec((tn, D_FOX), lambda b, j: (b, 0)),
            pl.BlockSpec((tn, D_FOX), lambda b, j: (b, 0)),
            pl.BlockSpec((1, FOX_HEADS, LANES), lambda b, j: (b, 0, 0)),
            _const_spec((rows, D_FOX)), _const_spec((rows, 1)),
        ],
        out_specs=pl.BlockSpec((tn, D_FOX), lambda b, j: (b, 0)),
        out_shape=jax.ShapeDtypeStruct((batch * tn, D_FOX), BF16),
        scratch_shapes=[
            pltpu.VMEM((rows, D_FOX), BF16),
            pltpu.VMEM((rows, 1), F32),
            pltpu.VMEM((rows, 1), F32),
            pltpu.VMEM((rows, D_FOX), F32),
            pltpu.VMEM((FOX_HEADS, LANES), F32),
        ],
        compiler_params=pltpu.CompilerParams(dimension_semantics=("arbitrary", "arbitrary"),
                                             vmem_limit_bytes=VMEM_LIMIT),
        name="fox_sample",
    )(q, kc, vc, lfc_t, kn, vn, lfn_t, own, qpos)


def _out_mlp_kernel(x_ref, ys_ref, yf_ref, wos_ref, wof_ref, g_ref, wup_ref, wdn_ref, o_ref):
    x1 = (x_ref[...]
          + jnp.dot(ys_ref[...], wos_ref[...], preferred_element_type=F32)
          + jnp.dot(yf_ref[...], wof_ref[...], preferred_element_type=F32))
    ms = jnp.mean(x1 * x1, axis=-1, keepdims=True)
    h = (x1 * lax.rsqrt(ms + EPS) * g_ref[...]).astype(BF16)
    up = jnp.dot(h, wup_ref[...], preferred_element_type=F32)
    a = jnp.square(jnp.maximum(up, 0.0)).astype(BF16)
    o_ref[...] = x1 + jnp.dot(a, wdn_ref[...], preferred_element_type=F32)


def _out_mlp(x2d, ys, yf, wo_s, wo_f, g2, w_up, w_dn, *, tm):
    m, d = x2d.shape
    d_ff = w_up.shape[1]
    row = lambda i: (i, 0)
    return pl.pallas_call(
        _out_mlp_kernel,
        grid=(m // tm,),
        in_specs=[
            pl.BlockSpec((tm, d), row), pl.BlockSpec((tm, D_SSM), row), pl.BlockSpec((tm, D_FOX), row),
            _const_spec((D_SSM, d)), _const_spec((D_FOX, d)), _const_spec((1, d)),
            _const_spec((d, d_ff)), _const_spec((d_ff, d)),
        ],
        out_specs=pl.BlockSpec((tm, d), row),
        out_shape=jax.ShapeDtypeStruct((m, d), F32),
        compiler_params=pltpu.CompilerParams(dimension_semantics=("arbitrary",), vmem_limit_bytes=VMEM_LIMIT),
        name="out_mlp",
    )(x2d, ys, yf, wo_s, wo_f, g2, w_up, w_dn)


def _state_to_kernel_layout(st):
    b = st.shape[0]
    st = st.reshape(b, SSD_GROUPS, HEADS_PER_GROUP, SSD_HEAD_DIM, D_STATE)
    return st.transpose(0, 1, 4, 2, 3).reshape(b, SSD_GROUPS, D_STATE, GROUP_W)


def _state_from_kernel_layout(st):
    b = st.shape[0]
    st = st.reshape(b, SSD_GROUPS, D_STATE, HEADS_PER_GROUP, SSD_HEAD_DIM)
    return st.transpose(0, 1, 3, 4, 2).reshape(b, SSD_HEADS, SSD_HEAD_DIM, D_STATE)


def _layer_weights(norm1_w, w_in, conv_w, conv_b, dt_bias, A_log, D_skip, ssd_norm_w, f_bias,
                   q_norm_w, k_norm_w, w_out, norm2_w, w_up, w_down):
    d = w_in.shape[0]
    splits = [D_SSM, D_CONV, SSD_HEADS, D_FOX, D_FOX, D_FOX]
    idx = [sum(splits[:i + 1]) for i in range(len(splits))]
    w_z, w_xbc, w_dt, w_q, w_k, w_v, w_f = jnp.split(w_in, idx, axis=1)
    pad = jnp.zeros((d, LANES - SSD_HEADS - FOX_HEADS), w_in.dtype)
    w_all = jnp.concatenate(
        [w_z, w_xbc, jnp.repeat(w_dt, SSD_HEAD_DIM, axis=1), w_q, w_k, w_v, w_dt, w_f, pad], axis=1).astype(BF16)
    bias_all = jnp.concatenate(
        [jnp.repeat(dt_bias, SSD_HEAD_DIM), dt_bias, f_bias,
         jnp.zeros((LANES - SSD_HEADS - FOX_HEADS,), F32)]).reshape(1, D_SSM + LANES).astype(F32)
    head_id = jnp.arange(D_FOX) // FOX_HEAD_DIM
    bd = (head_id[:, None] == head_id[None, :]).astype(BF16)
    alog_all = jnp.concatenate(
        [jnp.repeat(A_log, SSD_HEAD_DIM), A_log, jnp.zeros((LANES - SSD_HEADS,), F32)]).reshape(1, D_SSM + LANES)
    r = jnp.arange(SSD_CHUNK)
    tri = (r[None, :] <= r[:, None]).astype(BF16)
    hh = jnp.arange(FOX_HEADS)
    place = jnp.zeros((AUG_TERMS, LANES, D_FOX), F32)
    for t in range(AUG_TERMS):
        place = place.at[t, SMALL_F + hh, (hh // 2) * LANES + (hh % 2) * AUG_TERMS + t].set(1.0)
    place = place.astype(BF16)
    return dict(
        g1=norm1_w.reshape(1, d).astype(F32), w_all=w_all, bias_all=bias_all, bd=bd,
        qw=jnp.tile(q_norm_w, FOX_HEADS).reshape(1, D_FOX).astype(F32),
        kw=jnp.tile(k_norm_w, FOX_HEADS).reshape(1, D_FOX).astype(F32),
        conv_w=conv_w.astype(F32), conv_b=conv_b.reshape(1, D_CONV).astype(F32), alog_all=alog_all.astype(F32),
        dskip=jnp.repeat(D_skip, SSD_HEAD_DIM).reshape(1, D_SSM).astype(F32),
        normw=ssd_norm_w.reshape(1, D_SSM).astype(F32), tri=tri, place=place,
        wo_s=w_out[:D_SSM].astype(BF16), wo_f=w_out[D_SSM:].astype(BF16),
        g2=norm2_w.reshape(1, d).astype(F32), w_up=w_up.astype(BF16), w_dn=w_down.astype(BF16),
    )


def _pick_tile(n, pref):
    t = min(pref, n)
    while n % t:
        t //= 2
    return t


def _trunk(x, conv_prev, ssm_prev, w, fox_fn, *, prompt):
    b, l, d = x.shape
    m = b * l
    x2d = x.reshape(m, d)
    tm = _pick_tile(l if prompt else m, 512)
    z, xbc, dt, q, kf, vf, small, ka, va = _in_proj(
        x2d, w["g1"], w["w_all"], w["bias_all"], w["bd"], w["qw"], w["kw"], w["place"],
        tm=tm, seq_len=l, prompt=prompt)
    y_ssd, st_out, conv_out = _ssd(xbc, dt, small, z, conv_prev, _state_to_kernel_layout(ssm_prev.astype(F32)),
                                   w["conv_w"], w["conv_b"], w["alog_all"], w["dskip"], w["normw"], w["tri"],
                                   batch=b, seq_len=l)
    y_fox = fox_fn(q, ka, va, small, tm)
    y = _out_mlp(x2d, y_ssd, y_fox, w["wo_s"], w["wo_f"], w["g2"], w["w_up"], w["w_dn"], tm=_pick_tile(m, 512))
    logf = small[:, SMALL_F:SMALL_F + FOX_HEADS].reshape(b, l, FOX_HEADS)
    states = (kf.reshape(b, l, FOX_HEADS, FOX_HEAD_DIM), vf.reshape(b, l, FOX_HEADS, FOX_HEAD_DIM), logf,
              _state_from_kernel_layout(st_out), conv_out[:, SUBLANES - (CONV_W - 1):, :])
    return y.reshape(b, l, d), states


def kernel(x_prompt, x_sample, cache_k, cache_v, cache_logf, state_ssm, state_conv, norm1_w, w_in, conv_w, conv_b, dt_bias, A_log, D_skip, ssd_norm_w, f_bias, q_norm_w, k_norm_w, w_out, norm2_w, w_up, w_down):
    depth = w_in.shape[0]
    b_p, l_p, _ = x_prompt.shape
    b_s, l_s, _ = x_sample.shape
    past = cache_k.shape[2]
    y_prompt, y_sample = x_prompt, x_sample
    p_states, s_states = [], []
    for layer in range(depth):
        w = _layer_weights(norm1_w[layer], w_in[layer], conv_w[layer], conv_b[layer], dt_bias[layer], A_log[layer],
                           D_skip[layer], ssd_norm_w[layer], f_bias[layer], q_norm_w[layer], k_norm_w[layer],
                           w_out[layer], norm2_w[layer], w_up[layer], w_down[layer])

        def fox_prompt_fn(q, kaug, vt, small, tm):
            return _fox_prompt(q, kaug, vt, batch=b_p, seq_len=l_p, t=tm)

        def fox_sample_fn(q, kb, vb, small, tm, layer=layer):
            lf_new = small[:, SMALL_F:SMALL_F + FOX_HEADS].reshape(b_s, l_s, FOX_HEADS).transpose(0, 2, 1)
            lf_new = jnp.pad(lf_new, ((0, 0), (0, 0), (0, LANES - l_s)))
            lf_cache = cache_logf[layer].astype(F32).transpose(0, 2, 1)
            return _fox_sample(q, cache_k[layer].transpose(0, 2, 3, 1), cache_v[layer].transpose(0, 2, 3, 1),
                               lf_cache, kb, vb, lf_new, batch=b_s, tn=l_s, past=past, tk=_pick_tile(past, 1024))

        conv0 = jnp.zeros((b_p, CONV_W - 1, D_CONV), F32)
        ssm0 = jnp.zeros((b_p, SSD_HEADS, SSD_HEAD_DIM, D_STATE), F32)
        y_prompt, st_p = _trunk(y_prompt, conv0, ssm0, w, fox_prompt_fn, prompt=True)
        y_sample, st_s = _trunk(y_sample, state_conv[layer], state_ssm[layer], w, fox_sample_fn, prompt=False)
        p_states.append(st_p)
        s_states.append(st_s)
    stack = lambda states, i: jnp.stack([s[i] for s in states])
    return (y_prompt, y_sample,
            stack(p_states, 0), stack(p_states, 1), stack(p_states, 2), stack(p_states, 3), stack(p_states, 4),
            stack(s_states, 0), stack(s_states, 1), stack(s_states, 2), stack(s_states, 3), stack(s_states, 4))
```

```python
import functools
import math

import jax
import jax.numpy as jnp
from jax import lax
from jax.experimental import pallas as pl
from jax.experimental.pallas import tpu as pltpu

F32 = jnp.float32
BF16 = jnp.bfloat16

EPS = 1e-6
SSD_HEADS = 16
SSD_HEAD_DIM = 64
SSD_GROUPS = 2
HEADS_PER_GROUP = SSD_HEADS // SSD_GROUPS
D_STATE = 128
CONV_W = 4
D_SSM = SSD_HEADS * SSD_HEAD_DIM
D_BC = SSD_GROUPS * D_STATE
D_CONV = D_SSM + 2 * D_BC
GROUP_W = D_SSM // SSD_GROUPS
FOX_HEADS = 8
FOX_HEAD_DIM = 64
D_FOX = FOX_HEADS * FOX_HEAD_DIM

LANES = 128
SUBLANES = 8
SSD_CHUNK = 128
PREV_ROWS = 16
SSD_STREAMS = 1
VMEM_LIMIT = 56 * 1024 * 1024

C_Z = 0
C_XBC = C_Z + D_SSM
C_Q = C_XBC + D_CONV
C_K = C_Q + D_FOX
C_V = C_K + D_FOX
C_SMALL = C_V + D_FOX
N_PROJ = C_SMALL + LANES
SMALL_DT = 0
SMALL_F = SSD_HEADS
KAUG_W = 2 * LANES
AUG_TERMS = 3
LOG2E = math.log2(math.e)
Q_SCALE = FOX_HEAD_DIM ** -0.5 * LOG2E
VT_ROWS = 2 * FOX_HEAD_DIM


def _softplus(x):
    return jnp.maximum(x, 0.0) + jnp.log1p(jnp.exp(-jnp.abs(x)))


def _silu(x):
    return x * (1.0 / (1.0 + jnp.exp(-x)))


def _lane_cumsum(x):
    n = x.shape[-1]
    lane = lax.broadcasted_iota(jnp.int32, x.shape, x.ndim - 1)
    k = 1
    while k < n:
        x = x + jnp.where(lane >= k, pltpu.roll(x, k, axis=x.ndim - 1), 0.0)
        k *= 2
    return x


def _const_spec(shape):
    zeros = (0,) * len(shape)
    return pl.BlockSpec(shape, lambda *_: zeros, pipeline_mode=pl.Buffered(1))


def _in_proj_kernel(x_ref, g_ref, w_ref, bias_ref, bd_ref, qw_ref, kw_ref, *rest, tiles_per_seq, prompt):
    if prompt:
        (place_ref, z_ref, xbc_ref, q_ref, kf_ref, vf_ref, small_ref, kaug_ref, vt_ref, carry_ref) = rest
    else:
        (z_ref, xbc_ref, q_ref, kf_ref, vf_ref, small_ref, kb_ref, vb_ref) = rest
    x = x_ref[...]
    ms = jnp.mean(x * x, axis=-1, keepdims=True)
    h = (x * lax.rsqrt(ms + EPS) * g_ref[...]).astype(BF16)

    def proj(lo, hi):
        return jnp.dot(h, w_ref[:, lo:hi], preferred_element_type=F32)

    z_ref[...] = proj(C_Z, C_XBC).astype(BF16)
    sm = proj(C_SMALL, N_PROJ) + bias_ref[...]
    lane = lax.broadcasted_iota(jnp.int32, sm.shape, 1)
    sm = jnp.where(lane < SMALL_F, _softplus(sm), -_softplus(-sm))
    small_ref[...] = sm
    if prompt:
        @pl.when(pl.program_id(0) % tiles_per_seq == 0)
        def _():
            carry_ref[...] = jnp.zeros_like(carry_ref)

        cs_t = _lane_cumsum(sm.T) + carry_ref[:, 0:1]
        carry_ref[...] = jnp.broadcast_to(cs_t[:, -1:], carry_ref.shape)
        negc = cs_t.T * (-LOG2E)
        hi = negc.astype(BF16)
        r1 = negc - hi.astype(F32)
        mid = r1.astype(BF16)
        lo = (r1 - mid.astype(F32)).astype(BF16)

    xbc_ref[...] = proj(C_XBC, C_Q).astype(BF16)

    def head_rms(t, w):
        ss = jnp.dot((t * t).astype(BF16), bd_ref[...], preferred_element_type=F32)
        return t * lax.rsqrt(ss * (1.0 / FOX_HEAD_DIM) + EPS) * w

    qn = head_rms(proj(C_Q, C_K), qw_ref[...])
    q_ref[...] = (qn * Q_SCALE).astype(BF16) if prompt else (qn * (FOX_HEAD_DIM ** -0.5)).astype(BF16)
    kn = head_rms(proj(C_K, C_V), kw_ref[...])
    v = proj(C_V, C_SMALL)
    kf_ref[...] = kn.reshape(kf_ref.shape)
    vf_ref[...] = v.reshape(vf_ref.shape)

    kb = kn.astype(BF16)
    if not prompt:
        kb_ref[...] = kb
        vb_ref[...] = v.astype(BF16)
        return

    for p in range(FOX_HEADS // 2):
        kaug_ref[:, p * KAUG_W:p * KAUG_W + LANES] = kb[:, p * LANES:(p + 1) * LANES]
    v_t = v.T
    tm = v_t.shape[1]
    sub = lax.broadcasted_iota(jnp.int32, (VT_ROWS - FOX_HEAD_DIM, tm), 0)
    ones_then_zeros = jnp.where(sub == 0, 1.0, 0.0)
    for hd in range(FOX_HEADS):
        blk = jnp.concatenate([v_t[hd * FOX_HEAD_DIM:(hd + 1) * FOX_HEAD_DIM], ones_then_zeros], axis=0)
        vt_ref[0, hd * VT_ROWS:(hd + 1) * VT_ROWS, :] = blk.astype(BF16)
    aug = (jnp.dot(hi, place_ref[0], preferred_element_type=F32)
           + jnp.dot(mid, place_ref[1], preferred_element_type=F32)
           + jnp.dot(lo, place_ref[2], preferred_element_type=F32)).astype(BF16)
    for p in range(FOX_HEADS // 2):
        kaug_ref[:, p * KAUG_W + LANES:(p + 1) * KAUG_W] = aug[:, p * LANES:(p + 1) * LANES]


def _in_proj(x2d, g, w_all, bias_all, bd, qw, kw, place, *, tm, seq_len, prompt):
    m, d = x2d.shape
    nt = m // tm
    row = lambda i: (i, 0)
    out_shape = [
        jax.ShapeDtypeStruct((m, D_SSM), BF16),
        jax.ShapeDtypeStruct((m, D_CONV), BF16),
        jax.ShapeDtypeStruct((m, D_FOX), BF16),
        jax.ShapeDtypeStruct((m, FOX_HEADS, FOX_HEAD_DIM), F32),
        jax.ShapeDtypeStruct((m, FOX_HEADS, FOX_HEAD_DIM), F32),
        jax.ShapeDtypeStruct((m, LANES), F32),
    ]
    row3 = lambda i: (i, 0, 0)
    out_specs = [
        pl.BlockSpec((tm, D_SSM), row), pl.BlockSpec((tm, D_CONV), row),
        pl.BlockSpec((tm, D_FOX), row),
        pl.BlockSpec((tm, FOX_HEADS, FOX_HEAD_DIM), row3), pl.BlockSpec((tm, FOX_HEADS, FOX_HEAD_DIM), row3),
        pl.BlockSpec((tm, LANES), row),
    ]
    in_specs = [
        pl.BlockSpec((tm, d), row),
        _const_spec((1, d)), _const_spec((d, N_PROJ)), _const_spec((1, LANES)),
        _const_spec((D_FOX, D_FOX)), _const_spec((1, D_FOX)), _const_spec((1, D_FOX)),
    ]
    args = [x2d, g, w_all, bias_all, bd, qw, kw]
    scratch = []
    if prompt:
        in_specs.append(_const_spec(place.shape))
        args.append(place)
        out_shape += [jax.ShapeDtypeStruct((m, (FOX_HEADS // 2) * KAUG_W), BF16),
                      jax.ShapeDtypeStruct((nt, FOX_HEADS * VT_ROWS, tm), BF16)]
        out_specs += [pl.BlockSpec((tm, (FOX_HEADS // 2) * KAUG_W), row),
                      pl.BlockSpec((1, FOX_HEADS * VT_ROWS, tm), lambda i: (i, 0, 0))]
        scratch.append(pltpu.VMEM((LANES, LANES), F32))
    else:
        out_shape += [jax.ShapeDtypeStruct((m, D_FOX), BF16), jax.ShapeDtypeStruct((m, D_FOX), BF16)]
        out_specs += [pl.BlockSpec((tm, D_FOX), row), pl.BlockSpec((tm, D_FOX), row)]
    kern = functools.partial(_in_proj_kernel, tiles_per_seq=max(seq_len // tm, 1), prompt=prompt)
    return pl.pallas_call(
        kern,
        grid=(nt,),
        in_specs=in_specs,
        out_specs=out_specs,
        out_shape=out_shape,
        scratch_shapes=scratch,
        compiler_params=pltpu.CompilerParams(dimension_semantics=("arbitrary",), vmem_limit_bytes=VMEM_LIMIT),
        name="in_proj",
    )(*args)


def _pad_rows(x, rows):
    if x.shape[0] == rows:
        return x
    return jnp.concatenate([x, jnp.zeros((rows - x.shape[0],) + x.shape[1:], x.dtype)], axis=0)


def _ssd_kernel(xbc_ref, small_ref, z_ref, convprev_ref, st0_ref,
                convw_ref, convb_ref, alog_ref, dskip_ref, normw_ref, tri_ref, shift_ref, expand_ref,
                y_ref, stout_ref, convout_ref,
                prev_scr, st_scr, ydiag_scr, *, lc, streams):
    for s in range(streams):
        _ssd_stream(s, xbc_ref, small_ref, z_ref, convprev_ref, st0_ref,
                    convw_ref, convb_ref, alog_ref, dskip_ref, normw_ref, tri_ref, shift_ref, expand_ref,
                    y_ref, stout_ref, convout_ref, prev_scr, st_scr, ydiag_scr, lc)


def _ssd_stream(s, xbc_ref, small_ref, z_ref, convprev_ref, st0_ref,
                convw_ref, convb_ref, alog_ref, dskip_ref, normw_ref, tri_ref, shift_ref, expand_ref,
                y_ref, stout_ref, convout_ref, prev_scr, st_scr, ydiag_scr, lc):
    c = pl.program_id(1)
    last_chunk = c == pl.num_programs(1) - 1
    P = SSD_CHUNK

    @pl.when(c == 0)
    def _():
        prev_scr[s] = convprev_ref[s]
        st_scr[s] = st0_ref[s]

    cur = xbc_ref[s]
    shifted = jnp.dot(shift_ref[...], jnp.concatenate([prev_scr[s], cur], axis=0),
                      preferred_element_type=F32)
    conv = convb_ref[...]
    for k in range(CONV_W):
        conv = conv + convw_ref[k:k + 1, :] * shifted[k * lc:(k + 1) * lc]
    tail = cur[lc - PREV_ROWS:lc]
    prev_scr[s, 0:2 * PREV_ROWS] = jnp.zeros((2 * PREV_ROWS, D_CONV), BF16)
    prev_scr[s, 2 * PREV_ROWS:3 * PREV_ROWS] = tail

    @pl.when(last_chunk)
    def _():
        convout_ref[s] = tail.astype(F32)[PREV_ROWS - SUBLANES:]

    act = _pad_rows(_silu(conv), P)
    dt_c = _pad_rows(small_ref[s], P)
    dt_hi = dt_c.astype(BF16)
    dt_r = dt_c - dt_hi.astype(F32)
    dt_mid = dt_r.astype(BF16)
    dt_lo = (dt_r - dt_mid.astype(F32)).astype(BF16)
    dt = jnp.dot(jnp.concatenate([dt_hi, dt_mid, dt_lo], axis=1), expand_ref[...],
                 preferred_element_type=F32)
    xs = act[:, :D_SSM]
    xdt = xs * dt

    a_all = -jnp.exp(alog_ref[...])
    da = jnp.concatenate([dt, dt_c], axis=1) * a_all
    da_hi = da.astype(BF16)
    da_lo = (da - da_hi.astype(F32)).astype(BF16)
    tri = tri_ref[...]
    acs_all = (jnp.dot(tri, da_hi, preferred_element_type=F32)
               + jnp.dot(tri, da_lo, preferred_element_type=F32))
    acs = acs_all[:, :D_SSM]
    acs_t = acs_all[:, D_SSM:].T
    a_last = acs[P - 1:P, :]
    ea = jnp.exp(acs)
    xt = (xdt * jnp.exp(a_last - acs)).astype(BF16)
    sd = jnp.exp(a_last)

    rowi = lax.broadcasted_iota(jnp.int32, (P, P), 0)
    coli = lax.broadcasted_iota(jnp.int32, (P, P), 1)
    tril = coli <= rowi
    lane = lax.broadcasted_iota(jnp.int32, (P, LANES), 1)
    first_half = lane < SSD_HEAD_DIM

    yoff = []
    for g in range(SSD_GROUPS):
        bg = act[:, D_SSM + g * D_STATE:D_SSM + (g + 1) * D_STATE]
        cg = act[:, D_SSM + D_BC + g * D_STATE:D_SSM + D_BC + (g + 1) * D_STATE]
        bgb = bg.astype(BF16)
        cgb = cg.astype(BF16)
        cb = lax.dot_general(cgb, bgb, (((1,), (1,)), ((), ())), preferred_element_type=F32)
        for pr in range(HEADS_PER_GROUP // 2):
            lo = g * GROUP_W + pr * LANES
            h0 = g * HEADS_PER_GROUP + 2 * pr
            slab = acs[:, lo:lo + LANES]
            rolled = pltpu.roll(slab, SSD_HEAD_DIM, axis=1)
            xpair = xdt[:, lo:lo + LANES]
            yhead = []
            for j in range(2):
                col = jnp.where(first_half, slab, rolled) if j == 0 else jnp.where(first_half, rolled, slab)
                seg = col - acs_t[h0 + j:h0 + j + 1, :]
                dec = jnp.exp(jnp.where(tril, seg, -jnp.inf))
                mh = (cb * dec).astype(BF16)
                keep = first_half if j == 0 else jnp.logical_not(first_half)
                xm = jnp.where(keep, xpair, 0.0).astype(BF16)
                yhead.append(jnp.dot(mh, xm, preferred_element_type=F32))
            ydiag_scr[s, :, lo:lo + LANES] = yhead[0] + yhead[1]
        gs = slice(g * GROUP_W, (g + 1) * GROUP_W)
        st = st_scr[s, g]
        yoff.append(jnp.dot(cgb, st.astype(BF16), preferred_element_type=F32) * ea[:, gs])
        st_scr[s, g] = st * sd[:, gs] + jnp.dot(bg.T.astype(BF16), xt[:, gs], preferred_element_type=F32)

    y = ydiag_scr[s] + jnp.concatenate(yoff, axis=1) + xs * dskip_ref[...]
    y = y[:lc] * _silu(z_ref[s].astype(F32))
    outs = []
    for g in range(SSD_GROUPS):
        yg = y[:, g * GROUP_W:(g + 1) * GROUP_W]
        outs.append(yg * lax.rsqrt(jnp.mean(yg * yg, axis=-1, keepdims=True) + EPS))
    y_ref[s] = (jnp.concatenate(outs, axis=1) * normw_ref[...]).astype(BF16)

    @pl.when(last_chunk)
    def _():
        stout_ref[s] = st_scr[s]


def _conv_shift_matrix(lc):
    tap, t = jnp.meshgrid(jnp.arange(CONV_W), jnp.arange(lc), indexing="ij")
    src = (t - (CONV_W - 1) + tap).reshape(-1)
    col = jnp.arange(3 * PREV_ROWS + lc)[None, :]
    in_block = (src[:, None] >= 0) & (col == 3 * PREV_ROWS + src[:, None])
    in_prev = (src[:, None] < 0) & (col < 3 * PREV_ROWS) & (col % PREV_ROWS == PREV_ROWS + src[:, None])
    return (in_block | in_prev).astype(BF16)


def _split_prev_rows(conv_prev):
    b = conv_prev.shape[0]
    p = jnp.concatenate([jnp.zeros((b, PREV_ROWS - (CONV_W - 1), D_CONV), F32), conv_prev.astype(F32)], axis=1)
    hi = p.astype(BF16)
    r1 = p - hi.astype(F32)
    mid = r1.astype(BF16)
    lo = (r1 - mid.astype(F32)).astype(BF16)
    return jnp.concatenate([lo, mid, hi], axis=1)


def _ssd(xbc, small, z, conv_prev, st0, conv_w, conv_b, alog_all, dskip, normw, tri, expand, *, batch, seq_len):
    lc = min(SSD_CHUNK, seq_len)
    assert lc >= PREV_ROWS and seq_len % lc == 0
    nc = seq_len // lc
    shift = _conv_shift_matrix(lc)
    conv_prev = _split_prev_rows(conv_prev)
    ns = SSD_STREAMS if batch % SSD_STREAMS == 0 else 1
    blk = lambda b, c: (b, c, 0)
    per_b3 = lambda b, c: (b, 0, 0)
    per_b4 = lambda b, c: (b, 0, 0, 0)
    per_stream = lambda a: a.reshape(batch, seq_len, a.shape[-1])
    y, st_out, conv_out = pl.pallas_call(
        functools.partial(_ssd_kernel, lc=lc, streams=ns),
        grid=(batch // ns, nc),
        in_specs=[
            pl.BlockSpec((ns, lc, D_CONV), blk),
            pl.BlockSpec((ns, lc, LANES), blk), pl.BlockSpec((ns, lc, D_SSM), blk),
            pl.BlockSpec((ns, 3 * PREV_ROWS, D_CONV), per_b3),
            pl.BlockSpec((ns, SSD_GROUPS, D_STATE, GROUP_W), per_b4),
            _const_spec((CONV_W, D_CONV)), _const_spec((1, D_CONV)), _const_spec((1, D_SSM + LANES)),
            _const_spec((1, D_SSM)), _const_spec((1, D_SSM)), _const_spec((SSD_CHUNK, SSD_CHUNK)),
            _const_spec(shift.shape), _const_spec(expand.shape),
        ],
        out_specs=[
            pl.BlockSpec((ns, lc, D_SSM), blk),
            pl.BlockSpec((ns, SSD_GROUPS, D_STATE, GROUP_W), per_b4),
            pl.BlockSpec((ns, SUBLANES, D_CONV), per_b3),
        ],
        out_shape=[
            jax.ShapeDtypeStruct((batch, seq_len, D_SSM), BF16),
            jax.ShapeDtypeStruct((batch, SSD_GROUPS, D_STATE, GROUP_W), F32),
            jax.ShapeDtypeStruct((batch, SUBLANES, D_CONV), F32),
        ],
        scratch_shapes=[
            pltpu.VMEM((ns, 3 * PREV_ROWS, D_CONV), BF16),
            pltpu.VMEM((ns, SSD_GROUPS, D_STATE, GROUP_W), F32),
            pltpu.VMEM((ns, SSD_CHUNK, D_SSM), F32),
        ],
        compiler_params=pltpu.CompilerParams(dimension_semantics=("arbitrary", "arbitrary"),
                                             vmem_limit_bytes=VMEM_LIMIT),
        name="ssd",
    )(per_stream(xbc), per_stream(small), per_stream(z), conv_prev, st0,
      conv_w, conv_b, alog_all, dskip, normw, tri, shift, expand)
    return y.reshape(batch * seq_len, D_SSM), st_out, conv_out


def _fox_prompt_kernel(q_ref, ka_ref, vt_ref, o_ref, qa_scr, sta_scr, stb_scr, acc_scr, *, t):
    qi = pl.program_id(2)
    lane = lax.broadcasted_iota(jnp.int32, (t, LANES), 1)
    q = q_ref[...]
    for j in range(2):
        own = (lane < FOX_HEAD_DIM) if j == 0 else (lane >= FOX_HEAD_DIM)
        pick = jnp.where(lane >= AUG_TERMS * j, jnp.where(lane < AUG_TERMS * (j + 1), 1.0, 0.0), 0.0)
        qa_scr[j] = jnp.concatenate([jnp.where(own, q, jnp.zeros_like(q)), pick.astype(BF16)], axis=1)
    acc_scr[...] = jnp.zeros(acc_scr.shape, F32)

    def logits(ki, st_scr):
        start = pl.multiple_of(ki * t, t)
        ka = ka_ref[pl.ds(start, t), :]
        for j in range(2):
            st_scr[j] = lax.dot_general(ka, qa_scr[j], (((1,), (1,)), ((), ())),
                                        preferred_element_type=F32)

    def consume(ki, st_scr, m_prev, masked):
        out = []
        for j in range(2):
            st = st_scr[j]
            if masked:
                rowi = lax.broadcasted_iota(jnp.int32, (t, t), 0)
                coli = lax.broadcasted_iota(jnp.int32, (t, t), 1)
                st = jnp.where(rowi <= coli, st, -jnp.inf)
            m_new = jnp.maximum(m_prev[j], jnp.max(st, axis=0, keepdims=True))
            alpha = jnp.exp2(m_prev[j] - m_new)
            pexp = jnp.exp2(st - m_new).astype(BF16)
            vt = vt_ref[ki, j * VT_ROWS:(j + 1) * VT_ROWS, :]
            acc_scr[j] = alpha * acc_scr[j] + jnp.dot(vt, pexp, preferred_element_type=F32)
            out.append(m_new)
        return tuple(out)

    logits(0, sta_scr)

    def pair(k0, m):
        logits(k0 + 1, stb_scr)
        m = consume(k0, sta_scr, m, False)
        logits(k0 + 2, sta_scr)
        return consume(k0 + 1, stb_scr, m, False)

    quads = qi // 4
    m0 = tuple(jnp.full((1, t), -jnp.inf, F32) for _ in range(2))
    m = lax.fori_loop(0, quads, lambda i, m: pair(4 * i + 2, pair(4 * i, m)), m0)
    m = lax.fori_loop(0, (qi - 4 * quads) // 2, lambda i, m: pair(4 * quads + 2 * i, m), m)
    odd = lax.rem(qi, 2) == 1

    @pl.when(jnp.logical_not(odd))
    def _():
        consume(qi, sta_scr, m, True)

    @pl.when(odd)
    def _():
        logits(qi, stb_scr)
        m1 = consume(qi - 1, sta_scr, m, False)
        consume(qi, stb_scr, m1, True)

    halves = []
    for j in range(2):
        acc = acc_scr[j]
        halves.append(acc[:FOX_HEAD_DIM] * (1.0 / acc[FOX_HEAD_DIM:FOX_HEAD_DIM + 1]))
    o_ref[...] = jnp.concatenate(halves, axis=0).T.astype(BF16)


def _fox_prompt(q, kaug, vt, *, batch, seq_len, t):
    nq = seq_len // t
    m = batch * seq_len
    return pl.pallas_call(
        functools.partial(_fox_prompt_kernel, t=t),
        grid=(batch, FOX_HEADS // 2, nq),
        in_specs=[
            pl.BlockSpec((t, LANES), lambda b, p, i: (b * nq + i, p)),
            pl.BlockSpec((seq_len, KAUG_W), lambda b, p, i: (b, p)),
            pl.BlockSpec((nq, 2 * VT_ROWS, t), lambda b, p, i: (b, p, 0)),
        ],
        out_specs=pl.BlockSpec((t, LANES), lambda b, p, i: (b * nq + i, p)),
        out_shape=jax.ShapeDtypeStruct((m, D_FOX), BF16),
        scratch_shapes=[
            pltpu.VMEM((2, t, KAUG_W), BF16),
            pltpu.VMEM((2, t, t), F32),
            pltpu.VMEM((2, t, t), F32),
            pltpu.VMEM((2, VT_ROWS, t), F32),
        ],
        compiler_params=pltpu.CompilerParams(dimension_semantics=("arbitrary", "arbitrary", "arbitrary"),
                                             vmem_limit_bytes=VMEM_LIMIT),
        name="fox_prompt",
    )(q, kaug, vt)


def _fox_sample_kernel(q_ref, kc_ref, vc_ref, lfc_ref, kn_ref, vn_ref, lfn_ref, own_ref, qpos_ref, o_ref,
                       qbd_scr, m_scr, l_scr, acc_scr, carry_scr, *, tn):
    j = pl.program_id(1)
    nk = pl.num_programs(1)
    rows = FOX_HEADS * tn
    own = own_ref[...] > 0.0

    @pl.when(j == 0)
    def _():
        qt = jnp.concatenate([q_ref[...]] * FOX_HEADS, axis=0)
        qbd_scr[...] = jnp.where(own, qt, jnp.zeros_like(qt))
        m_scr[...] = jnp.full(m_scr.shape, -jnp.inf, F32)
        l_scr[...] = jnp.zeros(l_scr.shape, F32)
        acc_scr[...] = jnp.zeros(acc_scr.shape, F32)
        carry_scr[...] = jnp.zeros(carry_scr.shape, F32)

    def expand(ct):
        n = ct.shape[-1]
        return jnp.concatenate([jnp.broadcast_to(ct[h:h + 1], (tn, n)) for h in range(FOX_HEADS)], axis=0)

    nt_dims = (((1,), (1,)), ((), ()))

    def update(s, vals, vals_transposed):
        m_prev = m_scr[...]
        m_new = jnp.maximum(m_prev, jnp.max(s, axis=-1, keepdims=True))
        alpha = jnp.exp(m_prev - m_new)
        pexp = jnp.exp(s - m_new)
        l_scr[...] = alpha * l_scr[...] + jnp.sum(pexp, axis=-1, keepdims=True)
        if vals_transposed:
            pv = lax.dot_general(pexp.astype(BF16), vals, nt_dims, preferred_element_type=F32)
        else:
            pv = jnp.dot(pexp.astype(BF16), vals, preferred_element_type=F32)
        acc_scr[...] = alpha * acc_scr[...] + pv
        m_scr[...] = m_new

    ct = _lane_cumsum(lfc_ref[0]) + carry_scr[:, 0:1]
    carry_scr[...] = jnp.broadcast_to(ct[:, -1:], carry_scr.shape)
    tk = kc_ref.shape[-1]
    kt = kc_ref[0].reshape(D_FOX, tk).astype(BF16)
    s = jnp.dot(qbd_scr[...], kt, preferred_element_type=F32) - expand(ct)
    update(s, vc_ref[0].reshape(D_FOX, tk).astype(BF16), True)

    @pl.when(j == nk - 1)
    def _():
        cn = _lane_cumsum(lfn_ref[0]) + carry_scr[:, 0:1]
        kn = _pad_rows(kn_ref[...], LANES)
        vn = _pad_rows(vn_ref[...], LANES)
        sn = lax.dot_general(qbd_scr[...], kn, nt_dims, preferred_element_type=F32) - expand(cn)
        cidx = lax.broadcasted_iota(jnp.int32, (rows, LANES), 1)
        sn = jnp.where(cidx <= qpos_ref[...], sn, -jnp.inf)
        update(sn, vn, False)
        o = acc_scr[...] / l_scr[...]
        o = jnp.where(own, o, 0.0)
        out = o[0:tn]
        for h in range(1, FOX_HEADS):
            out = out + o[h * tn:(h + 1) * tn]
        o_ref[...] = out.astype(BF16)


def _fox_sample(q, kc, vc, lfc_t, kn, vn, lfn_t, *, batch, tn, past, tk):
    nk = past // tk
    rows = FOX_HEADS * tn
    r = jnp.arange(rows)
    own = ((r // tn)[:, None] == (jnp.arange(D_FOX) // FOX_HEAD_DIM)[None, :]).astype(F32)
    qpos = (r % tn).astype(jnp.int32).reshape(rows, 1)
    return pl.pallas_call(
        functools.partial(_fox_sample_kernel, tn=tn),
        grid=(batch, nk),
        in_specs=[
            pl.BlockSpec((tn, D_FOX), lambda b, j: (b, 0)),
            pl.BlockSpec((1, FOX_HEADS, FOX_HEAD_DIM, tk), lambda b, j: (b, 0, 0, j)),
            pl.BlockSpec((1, FOX_HEADS, FOX_HEAD_DIM, tk), lambda b, j: (b, 0, 0, j)),
            pl.BlockSpec((1, FOX_HEADS, tk), lambda b, j: (b, 0, j)),
            pl.BlockSpec((tn, D_FOX), lambda b, j: (b, 0)),
            pl.BlockSpec((tn, D_FOX), lambda b, j: (b, 0)),
            pl.BlockSpec((1, FOX_HEADS, LANES), lambda b, j: (b, 0, 0)),
            _const_spec((rows, D_FOX)), _const_spec((rows, 1)),
        ],
        out_specs=pl.BlockSpec((tn, D_FOX), lambda b, j: (b, 0)),
        out_shape=jax.ShapeDtypeStruct((batch * tn, D_FOX), BF16),
        scratch_shapes=[
            pltpu.VMEM((rows, D_FOX), BF16),
            pltpu.VMEM((rows, 1), F32),
            pltpu.VMEM((rows, 1), F32),
            pltpu.VMEM((rows, D_FOX), F32),
            pltpu.VMEM((FOX_HEADS, LANES), F32),
        ],
        compiler_params=pltpu.CompilerParams(dimension_semantics=("arbitrary", "arbitrary"),
                                             vmem_limit_bytes=VMEM_LIMIT),
        name="fox_sample",
    )(q, kc, vc, lfc_t, kn, vn, lfn_t, own, qpos)


def _out_mlp_kernel(x_ref, ys_ref, yf_ref, wos_ref, wof_ref, g_ref, wup_ref, wdn_ref, o_ref):
    x1 = (x_ref[...]
          + jnp.dot(ys_ref[...], wos_ref[...], preferred_element_type=F32)
          + jnp.dot(yf_ref[...], wof_ref[...], preferred_element_type=F32))
    ms = jnp.mean(x1 * x1, axis=-1, keepdims=True)
    h = (x1 * lax.rsqrt(ms + EPS) * g_ref[...]).astype(BF16)
    up = jnp.dot(h, wup_ref[...], preferred_element_type=F32)
    a = jnp.square(jnp.maximum(up, 0.0)).astype(BF16)
    o_ref[...] = x1 + jnp.dot(a, wdn_ref[...], preferred_element_type=F32)


def _out_mlp(x2d, ys, yf, wo_s, wo_f, g2, w_up, w_dn, *, tm):
    m, d = x2d.shape
    d_ff = w_up.shape[1]
    row = lambda i: (i, 0)
    return pl.pallas_call(
        _out_mlp_kernel,
        grid=(m // tm,),
        in_specs=[
            pl.BlockSpec((tm, d), row), pl.BlockSpec((tm, D_SSM), row), pl.BlockSpec((tm, D_FOX), row),
            _const_spec((D_SSM, d)), _const_spec((D_FOX, d)), _const_spec((1, d)),
            _const_spec((d, d_ff)), _const_spec((d_ff, d)),
        ],
        out_specs=pl.BlockSpec((tm, d), row),
        out_shape=jax.ShapeDtypeStruct((m, d), F32),
        compiler_params=pltpu.CompilerParams(dimension_semantics=("arbitrary",), vmem_limit_bytes=VMEM_LIMIT),
        name="out_mlp",
    )(x2d, ys, yf, wo_s, wo_f, g2, w_up, w_dn)


def _state_to_kernel_layout(st):
    b = st.shape[0]
    st = st.reshape(b, SSD_GROUPS, HEADS_PER_GROUP, SSD_HEAD_DIM, D_STATE)
    return st.transpose(0, 1, 4, 2, 3).reshape(b, SSD_GROUPS, D_STATE, GROUP_W)


def _state_from_kernel_layout(st):
    b = st.shape[0]
    st = st.reshape(b, SSD_GROUPS, D_STATE, HEADS_PER_GROUP, SSD_HEAD_DIM)
    return st.transpose(0, 1, 3, 4, 2).reshape(b, SSD_HEADS, SSD_HEAD_DIM, D_STATE)


def _layer_weights(norm1_w, w_in, conv_w, conv_b, dt_bias, A_log, D_skip, ssd_norm_w, f_bias,
                   q_norm_w, k_norm_w, w_out, norm2_w, w_up, w_down):
    d = w_in.shape[0]
    splits = [D_SSM, D_CONV, SSD_HEADS, D_FOX, D_FOX, D_FOX]
    idx = [sum(splits[:i + 1]) for i in range(len(splits))]
    w_z, w_xbc, w_dt, w_q, w_k, w_v, w_f = jnp.split(w_in, idx, axis=1)
    pad = jnp.zeros((d, LANES - SSD_HEADS - FOX_HEADS), w_in.dtype)
    w_all = jnp.concatenate(
        [w_z, w_xbc, w_q, w_k, w_v, w_dt, w_f, pad], axis=1).astype(BF16)
    bias_all = jnp.concatenate(
        [dt_bias, f_bias, jnp.zeros((LANES - SSD_HEADS - FOX_HEADS,), F32)]).reshape(1, LANES).astype(F32)
    src = jnp.arange(AUG_TERMS * LANES) % LANES
    expand = (src[:, None] == (jnp.arange(D_SSM) // SSD_HEAD_DIM)[None, :]).astype(BF16)
    head_id = jnp.arange(D_FOX) // FOX_HEAD_DIM
    bd = (head_id[:, None] == head_id[None, :]).astype(BF16)
    alog_all = jnp.concatenate(
        [jnp.repeat(A_log, SSD_HEAD_DIM), A_log, jnp.zeros((LANES - SSD_HEADS,), F32)]).reshape(1, D_SSM + LANES)
    r = jnp.arange(SSD_CHUNK)
    tri = (r[None, :] <= r[:, None]).astype(BF16)
    hh = jnp.arange(FOX_HEADS)
    place = jnp.zeros((AUG_TERMS, LANES, D_FOX), F32)
    for t in range(AUG_TERMS):
        place = place.at[t, SMALL_F + hh, (hh // 2) * LANES + (hh % 2) * AUG_TERMS + t].set(1.0)
    place = place.astype(BF16)
    return dict(
        g1=norm1_w.reshape(1, d).astype(F32), w_all=w_all, bias_all=bias_all, bd=bd, expand=expand,
        qw=jnp.tile(q_norm_w, FOX_HEADS).reshape(1, D_FOX).astype(F32),
        kw=jnp.tile(k_norm_w, FOX_HEADS).reshape(1, D_FOX).astype(F32),
        conv_w=conv_w.astype(F32), conv_b=conv_b.reshape(1, D_CONV).astype(F32), alog_all=alog_all.astype(F32),
        dskip=jnp.repeat(D_skip, SSD_HEAD_DIM).reshape(1, D_SSM).astype(F32),
        normw=ssd_norm_w.reshape(1, D_SSM).astype(F32), tri=tri, place=place,
        wo_s=w_out[:D_SSM].astype(BF16), wo_f=w_out[D_SSM:].astype(BF16),
        g2=norm2_w.reshape(1, d).astype(F32), w_up=w_up.astype(BF16), w_dn=w_down.astype(BF16),
    )


def _pick_tile(n, pref):
    t = min(pref, n)
    while n % t:
        t //= 2
    return t


def _trunk(x, conv_prev, ssm_prev, w, fox_fn, *, prompt):
    b, l, d = x.shape
    m = b * l
    x2d = x.reshape(m, d)
    tm = _pick_tile(l if prompt else m, 512)
    z, xbc, q, kf, vf, small, ka, va = _in_proj(
        x2d, w["g1"], w["w_all"], w["bias_all"], w["bd"], w["qw"], w["kw"], w["place"],
        tm=tm, seq_len=l, prompt=prompt)
    y_ssd, st_out, conv_out = _ssd(xbc, small, z, conv_prev, _state_to_kernel_layout(ssm_prev.astype(F32)),
                                   w["conv_w"], w["conv_b"], w["alog_all"], w["dskip"], w["normw"], w["tri"],
                                   w["expand"], batch=b, seq_len=l)
    y_fox = fox_fn(q, ka, va, small, tm)
    y = _out_mlp(x2d, y_ssd, y_fox, w["wo_s"], w["wo_f"], w["g2"], w["w_up"], w["w_dn"], tm=_pick_tile(m, 512))
    logf = small[:, SMALL_F:SMALL_F + FOX_HEADS].reshape(b, l, FOX_HEADS)
    states = (kf.reshape(b, l, FOX_HEADS, FOX_HEAD_DIM), vf.reshape(b, l, FOX_HEADS, FOX_HEAD_DIM), logf,
              _state_from_kernel_layout(st_out), conv_out[:, SUBLANES - (CONV_W - 1):, :])
    return y.reshape(b, l, d), states


def kernel(x_prompt, x_sample, cache_k, cache_v, cache_logf, state_ssm, state_conv, norm1_w, w_in, conv_w, conv_b, dt_bias, A_log, D_skip, ssd_norm_w, f_bias, q_norm_w, k_norm_w, w_out, norm2_w, w_up, w_down):
    depth = w_in.shape[0]
    b_p, l_p, _ = x_prompt.shape
    b_s, l_s, _ = x_sample.shape
    past = cache_k.shape[2]
    y_prompt, y_sample = x_prompt, x_sample
    p_states, s_states = [], []
    for layer in range(depth):
        w = _layer_weights(norm1_w[layer], w_in[layer], conv_w[layer], conv_b[layer], dt_bias[layer], A_log[layer],
                           D_skip[layer], ssd_norm_w[layer], f_bias[layer], q_norm_w[layer], k_norm_w[layer],
                           w_out[layer], norm2_w[layer], w_up[layer], w_down[layer])

        def fox_prompt_fn(q, kaug, vt, small, tm):
            return _fox_prompt(q, kaug, vt, batch=b_p, seq_len=l_p, t=tm)

        def fox_sample_fn(q, kb, vb, small, tm, layer=layer):
            lf_new = small[:, SMALL_F:SMALL_F + FOX_HEADS].reshape(b_s, l_s, FOX_HEADS).transpose(0, 2, 1)
            lf_new = jnp.pad(lf_new, ((0, 0), (0, 0), (0, LANES - l_s)))
            lf_cache = cache_logf[layer].astype(F32).transpose(0, 2, 1)
            return _fox_sample(q, cache_k[layer].transpose(0, 2, 3, 1), cache_v[layer].transpose(0, 2, 3, 1),
                               lf_cache, kb, vb, lf_new, batch=b_s, tn=l_s, past=past, tk=_pick_tile(past, 1024))

        conv0 = jnp.zeros((b_p, CONV_W - 1, D_CONV), F32)
        ssm0 = jnp.zeros((b_p, SSD_HEADS, SSD_HEAD_DIM, D_STATE), F32)
        y_prompt, st_p = _trunk(y_prompt, conv0, ssm0, w, fox_prompt_fn, prompt=True)
        y_sample, st_s = _trunk(y_sample, state_conv[layer], state_ssm[layer], w, fox_sample_fn, prompt=False)
        p_states.append(st_p)
        s_states.append(st_s)
    stack = lambda states, i: jnp.stack([s[i] for s in states])
    return (y_prompt, y_sample,
            stack(p_states, 0), stack(p_states, 1), stack(p_states, 2), stack(p_states, 3), stack(p_states, 4),
            stack(s_states, 0), stack(s_states, 1), stack(s_states, 2), stack(s_states, 3), stack(s_states, 4))
```

```python
import functools
import math

import jax
import jax.numpy as jnp
from jax import lax
from jax.experimental import pallas as pl
from jax.experimental.pallas import tpu as pltpu

F32 = jnp.float32
BF16 = jnp.bfloat16

EPS = 1e-6
SSD_HEADS = 16
SSD_HEAD_DIM = 64
SSD_GROUPS = 2
HEADS_PER_GROUP = SSD_HEADS // SSD_GROUPS
D_STATE = 128
CONV_W = 4
D_SSM = SSD_HEADS * SSD_HEAD_DIM
D_BC = SSD_GROUPS * D_STATE
D_CONV = D_SSM + 2 * D_BC
GROUP_W = D_SSM // SSD_GROUPS
FOX_HEADS = 8
FOX_HEAD_DIM = 64
D_FOX = FOX_HEADS * FOX_HEAD_DIM

LANES = 128
SUBLANES = 8
SSD_CHUNK = 256
PREV_ROWS = 16
SSD_STREAMS = 1
VMEM_LIMIT = 56 * 1024 * 1024

C_Z = 0
C_XBC = C_Z + D_SSM
C_Q = C_XBC + D_CONV
C_K = C_Q + D_FOX
C_V = C_K + D_FOX
C_SMALL = C_V + D_FOX
N_PROJ = C_SMALL + LANES
SMALL_DT = 0
SMALL_F = SSD_HEADS
KAUG_W = 2 * LANES
AUG_TERMS = 3
LOG2E = math.log2(math.e)
Q_SCALE = FOX_HEAD_DIM ** -0.5 * LOG2E
VT_ROWS = 2 * FOX_HEAD_DIM
PAIRS_PER_TRIP = 2


def _softplus(x):
    return jnp.maximum(x, 0.0) + jnp.log1p(jnp.exp(-jnp.abs(x)))


def _silu(x):
    return x * (1.0 / (1.0 + jnp.exp(-x)))


def _lane_cumsum(x):
    n = x.shape[-1]
    lane = lax.broadcasted_iota(jnp.int32, x.shape, x.ndim - 1)
    k = 1
    while k < n:
        x = x + jnp.where(lane >= k, pltpu.roll(x, k, axis=x.ndim - 1), 0.0)
        k *= 2
    return x


def _const_spec(shape):
    zeros = (0,) * len(shape)
    return pl.BlockSpec(shape, lambda *_: zeros, pipeline_mode=pl.Buffered(1))


def _in_proj_kernel(x_ref, g_ref, w_ref, bias_ref, bd_ref, qw_ref, kw_ref, *rest, tiles_per_seq, prompt):
    if prompt:
        (place_ref, z_ref, xbc_ref, q_ref, kf_ref, vf_ref, small_ref, kaug_ref, vt_ref, carry_ref) = rest
    else:
        (z_ref, xbc_ref, q_ref, kf_ref, vf_ref, small_ref, kb_ref, vb_ref) = rest
    x = x_ref[...]
    ms = jnp.mean(x * x, axis=-1, keepdims=True)
    h = (x * lax.rsqrt(ms + EPS) * g_ref[...]).astype(BF16)

    def proj(lo, hi):
        return jnp.dot(h, w_ref[:, lo:hi], preferred_element_type=F32)

    z_ref[...] = proj(C_Z, C_XBC).astype(BF16)
    sm = proj(C_SMALL, N_PROJ) + bias_ref[...]
    lane = lax.broadcasted_iota(jnp.int32, sm.shape, 1)
    sm = jnp.where(lane < SMALL_F, _softplus(sm), -_softplus(-sm))
    small_ref[...] = sm
    if prompt:
        @pl.when(pl.program_id(0) % tiles_per_seq == 0)
        def _():
            carry_ref[...] = jnp.zeros_like(carry_ref)

        cs_t = _lane_cumsum(sm.T) + carry_ref[:, 0:1]
        carry_ref[...] = jnp.broadcast_to(cs_t[:, -1:], carry_ref.shape)
        negc = cs_t.T * (-LOG2E)
        hi = negc.astype(BF16)
        r1 = negc - hi.astype(F32)
        mid = r1.astype(BF16)
        lo = (r1 - mid.astype(F32)).astype(BF16)

    xbc_ref[...] = proj(C_XBC, C_Q).astype(BF16)

    def head_rms(t, w):
        ss = jnp.dot((t * t).astype(BF16), bd_ref[...], preferred_element_type=F32)
        return t * lax.rsqrt(ss * (1.0 / FOX_HEAD_DIM) + EPS) * w

    qn = head_rms(proj(C_Q, C_K), qw_ref[...])
    q_ref[...] = (qn * Q_SCALE).astype(BF16) if prompt else (qn * (FOX_HEAD_DIM ** -0.5)).astype(BF16)
    kn = head_rms(proj(C_K, C_V), kw_ref[...])
    v = proj(C_V, C_SMALL)
    kf_ref[...] = kn.reshape(kf_ref.shape)
    vf_ref[...] = v.reshape(vf_ref.shape)

    kb = kn.astype(BF16)
    if not prompt:
        kb_ref[...] = kb
        vb_ref[...] = v.astype(BF16)
        return

    for p in range(FOX_HEADS // 2):
        kaug_ref[:, p * KAUG_W:p * KAUG_W + LANES] = kb[:, p * LANES:(p + 1) * LANES]
    v_t = v.T
    tm = v_t.shape[1]
    sub = lax.broadcasted_iota(jnp.int32, (VT_ROWS - FOX_HEAD_DIM, tm), 0)
    ones_then_zeros = jnp.where(sub == 0, 1.0, 0.0)
    for hd in range(FOX_HEADS):
        blk = jnp.concatenate([v_t[hd * FOX_HEAD_DIM:(hd + 1) * FOX_HEAD_DIM], ones_then_zeros], axis=0)
        vt_ref[0, hd * VT_ROWS:(hd + 1) * VT_ROWS, :] = blk.astype(BF16)
    aug = (jnp.dot(hi, place_ref[0], preferred_element_type=F32)
           + jnp.dot(mid, place_ref[1], preferred_element_type=F32)
           + jnp.dot(lo, place_ref[2], preferred_element_type=F32)).astype(BF16)
    for p in range(FOX_HEADS // 2):
        kaug_ref[:, p * KAUG_W + LANES:(p + 1) * KAUG_W] = aug[:, p * LANES:(p + 1) * LANES]


def _in_proj(x2d, g, w_all, bias_all, bd, qw, kw, place, *, tm, seq_len, prompt):
    m, d = x2d.shape
    nt = m // tm
    row = lambda i: (i, 0)
    out_shape = [
        jax.ShapeDtypeStruct((m, D_SSM), BF16),
        jax.ShapeDtypeStruct((m, D_CONV), BF16),
        jax.ShapeDtypeStruct((m, D_FOX), BF16),
        jax.ShapeDtypeStruct((m, FOX_HEADS, FOX_HEAD_DIM), F32),
        jax.ShapeDtypeStruct((m, FOX_HEADS, FOX_HEAD_DIM), F32),
        jax.ShapeDtypeStruct((m, LANES), F32),
    ]
    row3 = lambda i: (i, 0, 0)
    out_specs = [
        pl.BlockSpec((tm, D_SSM), row), pl.BlockSpec((tm, D_CONV), row),
        pl.BlockSpec((tm, D_FOX), row),
        pl.BlockSpec((tm, FOX_HEADS, FOX_HEAD_DIM), row3), pl.BlockSpec((tm, FOX_HEADS, FOX_HEAD_DIM), row3),
        pl.BlockSpec((tm, LANES), row),
    ]
    in_specs = [
        pl.BlockSpec((tm, d), row),
        _const_spec((1, d)), _const_spec((d, N_PROJ)), _const_spec((1, LANES)),
        _const_spec((D_FOX, D_FOX)), _const_spec((1, D_FOX)), _const_spec((1, D_FOX)),
    ]
    args = [x2d, g, w_all, bias_all, bd, qw, kw]
    scratch = []
    if prompt:
        in_specs.append(_const_spec(place.shape))
        args.append(place)
        out_shape += [jax.ShapeDtypeStruct((m, (FOX_HEADS // 2) * KAUG_W), BF16),
                      jax.ShapeDtypeStruct((nt, FOX_HEADS * VT_ROWS, tm), BF16)]
        out_specs += [pl.BlockSpec((tm, (FOX_HEADS // 2) * KAUG_W), row),
                      pl.BlockSpec((1, FOX_HEADS * VT_ROWS, tm), lambda i: (i, 0, 0))]
        scratch.append(pltpu.VMEM((LANES, LANES), F32))
    else:
        out_shape += [jax.ShapeDtypeStruct((m, D_FOX), BF16), jax.ShapeDtypeStruct((m, D_FOX), BF16)]
        out_specs += [pl.BlockSpec((tm, D_FOX), row), pl.BlockSpec((tm, D_FOX), row)]
    kern = functools.partial(_in_proj_kernel, tiles_per_seq=max(seq_len // tm, 1), prompt=prompt)
    return pl.pallas_call(
        kern,
        grid=(nt,),
        in_specs=in_specs,
        out_specs=out_specs,
        out_shape=out_shape,
        scratch_shapes=scratch,
        compiler_params=pltpu.CompilerParams(dimension_semantics=("arbitrary",), vmem_limit_bytes=VMEM_LIMIT),
        name="in_proj",
    )(*args)


def _pad_rows(x, rows):
    if x.shape[0] == rows:
        return x
    return jnp.concatenate([x, jnp.zeros((rows - x.shape[0],) + x.shape[1:], x.dtype)], axis=0)


def _ssd_kernel(xbc_ref, small_ref, z_ref, convprev_ref, st0_ref,
                convw_ref, convb_ref, alog_ref, dskip_ref, normw_ref, tri_ref, shift_ref, expand_ref,
                y_ref, stout_ref, convout_ref,
                prev_scr, st_scr, ydiag_scr, *, lc, streams):
    for s in range(streams):
        _ssd_stream(s, xbc_ref, small_ref, z_ref, convprev_ref, st0_ref,
                    convw_ref, convb_ref, alog_ref, dskip_ref, normw_ref, tri_ref, shift_ref, expand_ref,
                    y_ref, stout_ref, convout_ref, prev_scr, st_scr, ydiag_scr, lc)


def _ssd_stream(s, xbc_ref, small_ref, z_ref, convprev_ref, st0_ref,
                convw_ref, convb_ref, alog_ref, dskip_ref, normw_ref, tri_ref, shift_ref, expand_ref,
                y_ref, stout_ref, convout_ref, prev_scr, st_scr, ydiag_scr, lc):
    c = pl.program_id(1)
    last_chunk = c == pl.num_programs(1) - 1
    P = max(lc, LANES)

    @pl.when(c == 0)
    def _():
        prev_scr[s] = convprev_ref[s]
        st_scr[s] = st0_ref[s]

    cur = xbc_ref[s]
    sub = shift_ref.shape[0] // CONV_W
    prev = prev_scr[s]
    convs = []
    for i in range(lc // sub):
        blk = cur[i * sub:(i + 1) * sub]
        shifted = jnp.dot(shift_ref[...], jnp.concatenate([prev, blk], axis=0),
                          preferred_element_type=F32)
        conv = convb_ref[...]
        for k in range(CONV_W):
            conv = conv + convw_ref[k:k + 1, :] * shifted[k * sub:(k + 1) * sub]
        convs.append(conv)
        prev = jnp.concatenate([jnp.zeros((2 * PREV_ROWS, D_CONV), BF16), blk[sub - PREV_ROWS:]], axis=0)
    conv = convs[0] if len(convs) == 1 else jnp.concatenate(convs, axis=0)
    tail = cur[lc - PREV_ROWS:lc]
    prev_scr[s, 0:2 * PREV_ROWS] = jnp.zeros((2 * PREV_ROWS, D_CONV), BF16)
    prev_scr[s, 2 * PREV_ROWS:3 * PREV_ROWS] = tail

    @pl.when(last_chunk)
    def _():
        convout_ref[s] = tail.astype(F32)[PREV_ROWS - SUBLANES:]

    act = _pad_rows(_silu(conv), P)
    dt_c = _pad_rows(small_ref[s], P)
    dt_hi = dt_c.astype(BF16)
    dt_r = dt_c - dt_hi.astype(F32)
    dt_mid = dt_r.astype(BF16)
    dt_lo = (dt_r - dt_mid.astype(F32)).astype(BF16)
    dt = jnp.dot(jnp.concatenate([dt_hi, dt_mid, dt_lo], axis=1), expand_ref[...],
                 preferred_element_type=F32)
    xs = act[:, :D_SSM]
    xdt = xs * dt

    a_all = -jnp.exp(alog_ref[...])
    da = jnp.concatenate([dt, dt_c], axis=1) * a_all
    da_hi = da.astype(BF16)
    da_lo = (da - da_hi.astype(F32)).astype(BF16)
    tri = tri_ref[...]
    acs_all = (jnp.dot(tri, da_hi, preferred_element_type=F32)
               + jnp.dot(tri, da_lo, preferred_element_type=F32))
    acs = acs_all[:, :D_SSM]
    acs_t = acs_all[:, D_SSM:].T
    a_last = acs[P - 1:P, :]
    ea = jnp.exp(acs)
    xt = (xdt * jnp.exp(a_last - acs)).astype(BF16)
    sd = jnp.exp(a_last)

    rowi = lax.broadcasted_iota(jnp.int32, (P, P), 0)
    coli = lax.broadcasted_iota(jnp.int32, (P, P), 1)
    tril = coli <= rowi
    lane = lax.broadcasted_iota(jnp.int32, (P, LANES), 1)
    first_half = lane < SSD_HEAD_DIM

    yoff = []
    for g in range(SSD_GROUPS):
        bg = act[:, D_SSM + g * D_STATE:D_SSM + (g + 1) * D_STATE]
        cg = act[:, D_SSM + D_BC + g * D_STATE:D_SSM + D_BC + (g + 1) * D_STATE]
        bgb = bg.astype(BF16)
        cgb = cg.astype(BF16)
        cb = lax.dot_general(cgb, bgb, (((1,), (1,)), ((), ())), preferred_element_type=F32)
        for pr in range(HEADS_PER_GROUP // 2):
            lo = g * GROUP_W + pr * LANES
            h0 = g * HEADS_PER_GROUP + 2 * pr
            slab = acs[:, lo:lo + LANES]
            rolled = pltpu.roll(slab, SSD_HEAD_DIM, axis=1)
            xpair = xdt[:, lo:lo + LANES]
            yhead = []
            for j in range(2):
                col = jnp.where(first_half, slab, rolled) if j == 0 else jnp.where(first_half, rolled, slab)
                if P > LANES:
                    col = jnp.concatenate([col] * (P // LANES), axis=1)
                seg = col - acs_t[h0 + j:h0 + j + 1, :]
                dec = jnp.exp(jnp.where(tril, seg, -jnp.inf))
                mh = (cb * dec).astype(BF16)
                keep = first_half if j == 0 else jnp.logical_not(first_half)
                xm = jnp.where(keep, xpair, 0.0).astype(BF16)
                yhead.append(jnp.dot(mh, xm, preferred_element_type=F32))
            ydiag_scr[s, :, lo:lo + LANES] = yhead[0] + yhead[1]
        gs = slice(g * GROUP_W, (g + 1) * GROUP_W)
        st = st_scr[s, g]
        yoff.append(jnp.dot(cgb, st.astype(BF16), preferred_element_type=F32) * ea[:, gs])
        st_scr[s, g] = st * sd[:, gs] + jnp.dot(bg.T.astype(BF16), xt[:, gs], preferred_element_type=F32)

    y = ydiag_scr[s] + jnp.concatenate(yoff, axis=1) + xs * dskip_ref[...]
    y = y[:lc] * _silu(z_ref[s].astype(F32))
    outs = []
    for g in range(SSD_GROUPS):
        yg = y[:, g * GROUP_W:(g + 1) * GROUP_W]
        outs.append(yg * lax.rsqrt(jnp.mean(yg * yg, axis=-1, keepdims=True) + EPS))
    y_ref[s] = (jnp.concatenate(outs, axis=1) * normw_ref[...]).astype(BF16)

    @pl.when(last_chunk)
    def _():
        stout_ref[s] = st_scr[s]


def _conv_shift_matrix(lc):
    tap, t = jnp.meshgrid(jnp.arange(CONV_W), jnp.arange(lc), indexing="ij")
    src = (t - (CONV_W - 1) + tap).reshape(-1)
    col = jnp.arange(3 * PREV_ROWS + lc)[None, :]
    in_block = (src[:, None] >= 0) & (col == 3 * PREV_ROWS + src[:, None])
    in_prev = (src[:, None] < 0) & (col < 3 * PREV_ROWS) & (col % PREV_ROWS == PREV_ROWS + src[:, None])
    return (in_block | in_prev).astype(BF16)


def _split_prev_rows(conv_prev):
    b = conv_prev.shape[0]
    p = jnp.concatenate([jnp.zeros((b, PREV_ROWS - (CONV_W - 1), D_CONV), F32), conv_prev.astype(F32)], axis=1)
    hi = p.astype(BF16)
    r1 = p - hi.astype(F32)
    mid = r1.astype(BF16)
    lo = (r1 - mid.astype(F32)).astype(BF16)
    return jnp.concatenate([lo, mid, hi], axis=1)


def _ssd(xbc, small, z, conv_prev, st0, conv_w, conv_b, alog_all, dskip, normw, expand, *, batch, seq_len):
    lc = min(SSD_CHUNK, seq_len)
    assert lc >= PREV_ROWS and seq_len % lc == 0
    nc = seq_len // lc
    p = max(lc, LANES)
    r = jnp.arange(p)
    tri = (r[None, :] <= r[:, None]).astype(BF16)
    shift = _conv_shift_matrix(min(lc, LANES))
    conv_prev = _split_prev_rows(conv_prev)
    ns = SSD_STREAMS if batch % SSD_STREAMS == 0 else 1
    blk = lambda b, c: (b, c, 0)
    per_b3 = lambda b, c: (b, 0, 0)
    per_b4 = lambda b, c: (b, 0, 0, 0)
    per_stream = lambda a: a.reshape(batch, seq_len, a.shape[-1])
    y, st_out, conv_out = pl.pallas_call(
        functools.partial(_ssd_kernel, lc=lc, streams=ns),
        grid=(batch // ns, nc),
        in_specs=[
            pl.BlockSpec((ns, lc, D_CONV), blk),
            pl.BlockSpec((ns, lc, LANES), blk), pl.BlockSpec((ns, lc, D_SSM), blk),
            pl.BlockSpec((ns, 3 * PREV_ROWS, D_CONV), per_b3),
            pl.BlockSpec((ns, SSD_GROUPS, D_STATE, GROUP_W), per_b4),
            _const_spec((CONV_W, D_CONV)), _const_spec((1, D_CONV)), _const_spec((1, D_SSM + LANES)),
            _const_spec((1, D_SSM)), _const_spec((1, D_SSM)), _const_spec((p, p)),
            _const_spec(shift.shape), _const_spec(expand.shape),
        ],
        out_specs=[
            pl.BlockSpec((ns, lc, D_SSM), blk),
            pl.BlockSpec((ns, SSD_GROUPS, D_STATE, GROUP_W), per_b4),
            pl.BlockSpec((ns, SUBLANES, D_CONV), per_b3),
        ],
        out_shape=[
            jax.ShapeDtypeStruct((batch, seq_len, D_SSM), BF16),
            jax.ShapeDtypeStruct((batch, SSD_GROUPS, D_STATE, GROUP_W), F32),
            jax.ShapeDtypeStruct((batch, SUBLANES, D_CONV), F32),
        ],
        scratch_shapes=[
            pltpu.VMEM((ns, 3 * PREV_ROWS, D_CONV), BF16),
            pltpu.VMEM((ns, SSD_GROUPS, D_STATE, GROUP_W), F32),
            pltpu.VMEM((ns, p, D_SSM), F32),
        ],
        compiler_params=pltpu.CompilerParams(dimension_semantics=("arbitrary", "arbitrary"),
                                             vmem_limit_bytes=VMEM_LIMIT),
        name="ssd",
    )(per_stream(xbc), per_stream(small), per_stream(z), conv_prev, st0,
      conv_w, conv_b, alog_all, dskip, normw, tri, shift, expand)
    return y.reshape(batch * seq_len, D_SSM), st_out, conv_out


def _fox_prompt_kernel(q_ref, ka_ref, vt_ref, o_ref, qa_scr, sta_scr, stb_scr, acc_scr, *, t):
    qi = pl.program_id(2)
    lane = lax.broadcasted_iota(jnp.int32, (t, LANES), 1)
    q = q_ref[...]
    for j in range(2):
        own = (lane < FOX_HEAD_DIM) if j == 0 else (lane >= FOX_HEAD_DIM)
        pick = jnp.where(lane >= AUG_TERMS * j, jnp.where(lane < AUG_TERMS * (j + 1), 1.0, 0.0), 0.0)
        qa_scr[j] = jnp.concatenate([jnp.where(own, q, jnp.zeros_like(q)), pick.astype(BF16)], axis=1)
    acc_scr[...] = jnp.zeros(acc_scr.shape, F32)

    def logits(ki, st_scr):
        start = pl.multiple_of(ki * t, t)
        ka = ka_ref[pl.ds(start, t), :]
        for j in range(2):
            st_scr[j] = lax.dot_general(ka, qa_scr[j], (((1,), (1,)), ((), ())),
                                        preferred_element_type=F32)

    def consume(ki, st_scr, m_prev, masked):
        out = []
        for j in range(2):
            st = st_scr[j]
            if masked:
                rowi = lax.broadcasted_iota(jnp.int32, (t, t), 0)
                coli = lax.broadcasted_iota(jnp.int32, (t, t), 1)
                st = jnp.where(rowi <= coli, st, -jnp.inf)
            m_new = jnp.maximum(m_prev[j], jnp.max(st, axis=0, keepdims=True))
            alpha = jnp.exp2(m_prev[j] - m_new)
            pexp = jnp.exp2(st - m_new).astype(BF16)
            vt = vt_ref[ki, j * VT_ROWS:(j + 1) * VT_ROWS, :]
            acc_scr[j] = alpha * acc_scr[j] + jnp.dot(vt, pexp, preferred_element_type=F32)
            out.append(m_new)
        return tuple(out)

    logits(0, sta_scr)

    def pair(k0, m):
        logits(k0 + 1, stb_scr)
        m = consume(k0, sta_scr, m, False)
        logits(k0 + 2, sta_scr)
        return consume(k0 + 1, stb_scr, m, False)

    def pairs(k0, m):
        for u in range(PAIRS_PER_TRIP):
            m = pair(k0 + 2 * u, m)
        return m

    span = 2 * PAIRS_PER_TRIP
    trips = qi // span
    m0 = tuple(jnp.full((1, t), -jnp.inf, F32) for _ in range(2))
    m = lax.fori_loop(0, trips, lambda i, m: pairs(span * i, m), m0)
    m = lax.fori_loop(0, (qi - span * trips) // 2, lambda i, m: pair(span * trips + 2 * i, m), m)
    odd = lax.rem(qi, 2) == 1

    @pl.when(jnp.logical_not(odd))
    def _():
        consume(qi, sta_scr, m, True)

    @pl.when(odd)
    def _():
        logits(qi, stb_scr)
        m1 = consume(qi - 1, sta_scr, m, False)
        consume(qi, stb_scr, m1, True)

    halves = []
    for j in range(2):
        acc = acc_scr[j]
        halves.append(acc[:FOX_HEAD_DIM] * (1.0 / acc[FOX_HEAD_DIM:FOX_HEAD_DIM + 1]))
    o_ref[...] = jnp.concatenate(halves, axis=0).T.astype(BF16)


def _fox_prompt(q, kaug, vt, *, batch, seq_len, t):
    nq = seq_len // t
    m = batch * seq_len
    return pl.pallas_call(
        functools.partial(_fox_prompt_kernel, t=t),
        grid=(batch, FOX_HEADS // 2, nq),
        in_specs=[
            pl.BlockSpec((t, LANES), lambda b, p, i: (b * nq + i, p)),
            pl.BlockSpec((seq_len, KAUG_W), lambda b, p, i: (b, p)),
            pl.BlockSpec((nq, 2 * VT_ROWS, t), lambda b, p, i: (b, p, 0)),
        ],
        out_specs=pl.BlockSpec((t, LANES), lambda b, p, i: (b * nq + i, p)),
        out_shape=jax.ShapeDtypeStruct((m, D_FOX), BF16),
        scratch_shapes=[
            pltpu.VMEM((2, t, KAUG_W), BF16),
            pltpu.VMEM((2, t, t), F32),
            pltpu.VMEM((2, t, t), F32),
            pltpu.VMEM((2, VT_ROWS, t), F32),
        ],
        compiler_params=pltpu.CompilerParams(dimension_semantics=("arbitrary", "arbitrary", "arbitrary"),
                                             vmem_limit_bytes=VMEM_LIMIT),
        name="fox_prompt",
    )(q, kaug, vt)


def _fox_sample_kernel(q_ref, kc_ref, vc_ref, lfc_ref, kn_ref, vn_ref, lfn_ref, own_ref, qpos_ref, o_ref,
                       qbd_scr, m_scr, l_scr, acc_scr, carry_scr, *, tn):
    j = pl.program_id(1)
    nk = pl.num_programs(1)
    rows = FOX_HEADS * tn
    own = own_ref[...] > 0.0

    @pl.when(j == 0)
    def _():
        qt = jnp.concatenate([q_ref[...]] * FOX_HEADS, axis=0)
        qbd_scr[...] = jnp.where(own, qt, jnp.zeros_like(qt))
        m_scr[...] = jnp.full(m_scr.shape, -jnp.inf, F32)
        l_scr[...] = jnp.zeros(l_scr.shape, F32)
        acc_scr[...] = jnp.zeros(acc_scr.shape, F32)
        carry_scr[...] = jnp.zeros(carry_scr.shape, F32)

    def expand(ct):
        n = ct.shape[-1]
        return jnp.concatenate([jnp.broadcast_to(ct[h:h + 1], (tn, n)) for h in range(FOX_HEADS)], axis=0)

    nt_dims = (((1,), (1,)), ((), ()))

    def update(s, vals, vals_transposed):
        m_prev = m_scr[...]
        m_new = jnp.maximum(m_prev, jnp.max(s, axis=-1, keepdims=True))
        alpha = jnp.exp(m_prev - m_new)
        pexp = jnp.exp(s - m_new)
        l_scr[...] = alpha * l_scr[...] + jnp.sum(pexp, axis=-1, keepdims=True)
        if vals_transposed:
            pv = lax.dot_general(pexp.astype(BF16), vals, nt_dims, preferred_element_type=F32)
        else:
            pv = jnp.dot(pexp.astype(BF16), vals, preferred_element_type=F32)
        acc_scr[...] = alpha * acc_scr[...] + pv
        m_scr[...] = m_new

    ct = _lane_cumsum(lfc_ref[0]) + carry_scr[:, 0:1]
    carry_scr[...] = jnp.broadcast_to(ct[:, -1:], carry_scr.shape)
    tk = kc_ref.shape[-1]
    kt = kc_ref[0].reshape(D_FOX, tk).astype(BF16)
    s = jnp.dot(qbd_scr[...], kt, preferred_element_type=F32) - expand(ct)
    update(s, vc_ref[0].reshape(D_FOX, tk).astype(BF16), True)

    @pl.when(j == nk - 1)
    def _():
        cn = _lane_cumsum(lfn_ref[0]) + carry_scr[:, 0:1]
        kn = _pad_rows(kn_ref[...], LANES)
        vn = _pad_rows(vn_ref[...], LANES)
        sn = lax.dot_general(qbd_scr[...], kn, nt_dims, preferred_element_type=F32) - expand(cn)
        cidx = lax.broadcasted_iota(jnp.int32, (rows, LANES), 1)
        sn = jnp.where(cidx <= qpos_ref[...], sn, -jnp.inf)
        update(sn, vn, False)
        o = acc_scr[...] / l_scr[...]
        o = jnp.where(own, o, 0.0)
        out = o[0:tn]
        for h in range(1, FOX_HEADS):
            out = out + o[h * tn:(h + 1) * tn]
        o_ref[...] = out.astype(BF16)


def _fox_sample(q, kc, vc, lfc_t, kn, vn, lfn_t, *, batch, tn, past, tk):
    nk = past // tk
    rows = FOX_HEADS * tn
    r = jnp.arange(rows)
    own = ((r // tn)[:, None] == (jnp.arange(D_FOX) // FOX_HEAD_DIM)[None, :]).astype(F32)
    qpos = (r % tn).astype(jnp.int32).reshape(rows, 1)
    return pl.pallas_call(
        functools.partial(_fox_sample_kernel, tn=tn),
        grid=(batch, nk),
        in_specs=[
            pl.BlockSpec((tn, D_FOX), lambda b, j: (b, 0)),
            pl.BlockSpec((1, FOX_HEADS, FOX_HEAD_DIM, tk), lambda b, j: (b, 0, 0, j)),
            pl.BlockSpec((1, FOX_HEADS, FOX_HEAD_DIM, tk), lambda b, j: (b, 0, 0, j)),
            pl.BlockSpec((1, FOX_HEADS, tk), lambda b, j: (b, 0, j)),
            pl.BlockSpec((tn, D_FOX), lambda b, j: (b, 0)),
            pl.BlockSpec((tn, D_FOX), lambda b, j: (b, 0)),
            pl.BlockSpec((1, FOX_HEADS, LANES), lambda b, j: (b, 0, 0)),
            _const_spec((rows, D_FOX)), _const_spec((rows, 1)),
        ],
        out_specs=pl.BlockSpec((tn, D_FOX), lambda b, j: (b, 0)),
        out_shape=jax.ShapeDtypeStruct((batch * tn, D_FOX), BF16),
        scratch_shapes=[
            pltpu.VMEM((rows, D_FOX), BF16),
            pltpu.VMEM((rows, 1), F32),
            pltpu.VMEM((rows, 1), F32),
            pltpu.VMEM((rows, D_FOX), F32),
            pltpu.VMEM((FOX_HEADS, LANES), F32),
        ],
        compiler_params=pltpu.CompilerParams(dimension_semantics=("arbitrary", "arbitrary"),
                                             vmem_limit_bytes=VMEM_LIMIT),
        name="fox_sample",
    )(q, kc, vc, lfc_t, kn, vn, lfn_t, own, qpos)


def _out_mlp_kernel(x_ref, ys_ref, yf_ref, wos_ref, wof_ref, g_ref, wup_ref, wdn_ref, o_ref):
    x1 = (x_ref[...]
          + jnp.dot(ys_ref[...], wos_ref[...], preferred_element_type=F32)
          + jnp.dot(yf_ref[...], wof_ref[...], preferred_element_type=F32))
    ms = jnp.mean(x1 * x1, axis=-1, keepdims=True)
    h = (x1 * lax.rsqrt(ms + EPS) * g_ref[...]).astype(BF16)
    up = jnp.dot(h, wup_ref[...], preferred_element_type=F32)
    a = jnp.square(jnp.maximum(up, 0.0)).astype(BF16)
    o_ref[...] = x1 + jnp.dot(a, wdn_ref[...], preferred_element_type=F32)


def _out_mlp(x2d, ys, yf, wo_s, wo_f, g2, w_up, w_dn, *, tm):
    m, d = x2d.shape
    d_ff = w_up.shape[1]
    row = lambda i: (i, 0)
    return pl.pallas_call(
        _out_mlp_kernel,
        grid=(m // tm,),
        in_specs=[
            pl.BlockSpec((tm, d), row), pl.BlockSpec((tm, D_SSM), row), pl.BlockSpec((tm, D_FOX), row),
            _const_spec((D_SSM, d)), _const_spec((D_FOX, d)), _const_spec((1, d)),
            _const_spec((d, d_ff)), _const_spec((d_ff, d)),
        ],
        out_specs=pl.BlockSpec((tm, d), row),
        out_shape=jax.ShapeDtypeStruct((m, d), F32),
        compiler_params=pltpu.CompilerParams(dimension_semantics=("arbitrary",), vmem_limit_bytes=VMEM_LIMIT),
        name="out_mlp",
    )(x2d, ys, yf, wo_s, wo_f, g2, w_up, w_dn)


def _state_to_kernel_layout(st):
    b = st.shape[0]
    st = st.reshape(b, SSD_GROUPS, HEADS_PER_GROUP, SSD_HEAD_DIM, D_STATE)
    return st.transpose(0, 1, 4, 2, 3).reshape(b, SSD_GROUPS, D_STATE, GROUP_W)


def _state_from_kernel_layout(st):
    b = st.shape[0]
    st = st.reshape(b, SSD_GROUPS, D_STATE, HEADS_PER_GROUP, SSD_HEAD_DIM)
    return st.transpose(0, 1, 3, 4, 2).reshape(b, SSD_HEADS, SSD_HEAD_DIM, D_STATE)


def _layer_weights(norm1_w, w_in, conv_w, conv_b, dt_bias, A_log, D_skip, ssd_norm_w, f_bias,
                   q_norm_w, k_norm_w, w_out, norm2_w, w_up, w_down):
    d = w_in.shape[0]
    splits = [D_SSM, D_CONV, SSD_HEADS, D_FOX, D_FOX, D_FOX]
    idx = [sum(splits[:i + 1]) for i in range(len(splits))]
    w_z, w_xbc, w_dt, w_q, w_k, w_v, w_f = jnp.split(w_in, idx, axis=1)
    pad = jnp.zeros((d, LANES - SSD_HEADS - FOX_HEADS), w_in.dtype)
    w_all = jnp.concatenate(
        [w_z, w_xbc, w_q, w_k, w_v, w_dt, w_f, pad], axis=1).astype(BF16)
    bias_all = jnp.concatenate(
        [dt_bias, f_bias, jnp.zeros((LANES - SSD_HEADS - FOX_HEADS,), F32)]).reshape(1, LANES).astype(F32)
    src = jnp.arange(AUG_TERMS * LANES) % LANES
    expand = (src[:, None] == (jnp.arange(D_SSM) // SSD_HEAD_DIM)[None, :]).astype(BF16)
    head_id = jnp.arange(D_FOX) // FOX_HEAD_DIM
    bd = (head_id[:, None] == head_id[None, :]).astype(BF16)
    alog_all = jnp.concatenate(
        [jnp.repeat(A_log, SSD_HEAD_DIM), A_log, jnp.zeros((LANES - SSD_HEADS,), F32)]).reshape(1, D_SSM + LANES)
    hh = jnp.arange(FOX_HEADS)
    place = jnp.zeros((AUG_TERMS, LANES, D_FOX), F32)
    for t in range(AUG_TERMS):
        place = place.at[t, SMALL_F + hh, (hh // 2) * LANES + (hh % 2) * AUG_TERMS + t].set(1.0)
    place = place.astype(BF16)
    return dict(
        g1=norm1_w.reshape(1, d).astype(F32), w_all=w_all, bias_all=bias_all, bd=bd, expand=expand,
        qw=jnp.tile(q_norm_w, FOX_HEADS).reshape(1, D_FOX).astype(F32),
        kw=jnp.tile(k_norm_w, FOX_HEADS).reshape(1, D_FOX).astype(F32),
        conv_w=conv_w.astype(F32), conv_b=conv_b.reshape(1, D_CONV).astype(F32), alog_all=alog_all.astype(F32),
        dskip=jnp.repeat(D_skip, SSD_HEAD_DIM).reshape(1, D_SSM).astype(F32),
        normw=ssd_norm_w.reshape(1, D_SSM).astype(F32), place=place,
        wo_s=w_out[:D_SSM].astype(BF16), wo_f=w_out[D_SSM:].astype(BF16),
        g2=norm2_w.reshape(1, d).astype(F32), w_up=w_up.astype(BF16), w_dn=w_down.astype(BF16),
    )


def _pick_tile(n, pref):
    t = min(pref, n)
    while n % t:
        t //= 2
    return t


def _trunk(x, conv_prev, ssm_prev, w, fox_fn, *, prompt):
    b, l, d = x.shape
    m = b * l
    x2d = x.reshape(m, d)
    tm = _pick_tile(l if prompt else m, 512)
    z, xbc, q, kf, vf, small, ka, va = _in_proj(
        x2d, w["g1"], w["w_all"], w["bias_all"], w["bd"], w["qw"], w["kw"], w["place"],
        tm=tm, seq_len=l, prompt=prompt)
    y_ssd, st_out, conv_out = _ssd(xbc, small, z, conv_prev, _state_to_kernel_layout(ssm_prev.astype(F32)),
                                   w["conv_w"], w["conv_b"], w["alog_all"], w["dskip"], w["normw"],
                                   w["expand"], batch=b, seq_len=l)
    y_fox = fox_fn(q, ka, va, small, tm)
    y = _out_mlp(x2d, y_ssd, y_fox, w["wo_s"], w["wo_f"], w["g2"], w["w_up"], w["w_dn"], tm=_pick_tile(m, 512))
    logf = small[:, SMALL_F:SMALL_F + FOX_HEADS].reshape(b, l, FOX_HEADS)
    states = (kf.reshape(b, l, FOX_HEADS, FOX_HEAD_DIM), vf.reshape(b, l, FOX_HEADS, FOX_HEAD_DIM), logf,
              _state_from_kernel_layout(st_out), conv_out[:, SUBLANES - (CONV_W - 1):, :])
    return y.reshape(b, l, d), states


def kernel(x_prompt, x_sample, cache_k, cache_v, cache_logf, state_ssm, state_conv, norm1_w, w_in, conv_w, conv_b, dt_bias, A_log, D_skip, ssd_norm_w, f_bias, q_norm_w, k_norm_w, w_out, norm2_w, w_up, w_down):
    depth = w_in.shape[0]
    b_p, l_p, _ = x_prompt.shape
    b_s, l_s, _ = x_sample.shape
    past = cache_k.shape[2]
    y_prompt, y_sample = x_prompt, x_sample
    p_states, s_states = [], []
    for layer in range(depth):
        w = _layer_weights(norm1_w[layer], w_in[layer], conv_w[layer], conv_b[layer], dt_bias[layer], A_log[layer],
                           D_skip[layer], ssd_norm_w[layer], f_bias[layer], q_norm_w[layer], k_norm_w[layer],
                           w_out[layer], norm2_w[layer], w_up[layer], w_down[layer])

        def fox_prompt_fn(q, kaug, vt, small, tm):
            return _fox_prompt(q, kaug, vt, batch=b_p, seq_len=l_p, t=tm)

        def fox_sample_fn(q, kb, vb, small, tm, layer=layer):
            lf_new = small[:, SMALL_F:SMALL_F + FOX_HEADS].reshape(b_s, l_s, FOX_HEADS).transpose(0, 2, 1)
            lf_new = jnp.pad(lf_new, ((0, 0), (0, 0), (0, LANES - l_s)))
            lf_cache = cache_logf[layer].astype(F32).transpose(0, 2, 1)
            return _fox_sample(q, cache_k[layer].transpose(0, 2, 3, 1), cache_v[layer].transpose(0, 2, 3, 1),
                               lf_cache, kb, vb, lf_new, batch=b_s, tn=l_s, past=past, tk=_pick_tile(past, 2048))

        conv0 = jnp.zeros((b_p, CONV_W - 1, D_CONV), F32)
        ssm0 = jnp.zeros((b_p, SSD_HEADS, SSD_HEAD_DIM, D_STATE), F32)
        y_prompt, st_p = _trunk(y_prompt, conv0, ssm0, w, fox_prompt_fn, prompt=True)
        y_sample, st_s = _trunk(y_sample, state_conv[layer], state_ssm[layer], w, fox_sample_fn, prompt=False)
        p_states.append(st_p)
        s_states.append(st_s)
    stack = lambda states, i: jnp.stack([s[i] for s in states])
    return (y_prompt, y_sample,
            stack(p_states, 0), stack(p_states, 1), stack(p_states, 2), stack(p_states, 3), stack(p_states, 4),
            stack(s_states, 0), stack(s_states, 1), stack(s_states, 2), stack(s_states, 3), stack(s_states, 4))
```

```python
import functools
import math

import jax
import jax.numpy as jnp
from jax import lax
from jax.experimental import pallas as pl
from jax.experimental.pallas import tpu as pltpu

F32 = jnp.float32
BF16 = jnp.bfloat16

EPS = 1e-6
SSD_HEADS = 16
SSD_HEAD_DIM = 64
SSD_GROUPS = 2
HEADS_PER_GROUP = SSD_HEADS // SSD_GROUPS
D_STATE = 128
CONV_W = 4
D_SSM = SSD_HEADS * SSD_HEAD_DIM
D_BC = SSD_GROUPS * D_STATE
D_CONV = D_SSM + 2 * D_BC
GROUP_W = D_SSM // SSD_GROUPS
FOX_HEADS = 8
FOX_HEAD_DIM = 64
D_FOX = FOX_HEADS * FOX_HEAD_DIM

LANES = 128
SUBLANES = 8
SSD_CHUNK = 256
PREV_ROWS = 16
SSD_STREAMS = 1
VMEM_LIMIT = 56 * 1024 * 1024

C_Z = 0
C_XBC = C_Z + D_SSM
C_Q = C_XBC + D_CONV
C_K = C_Q + D_FOX
C_V = C_K + D_FOX
C_SMALL = C_V + D_FOX
N_PROJ = C_SMALL + LANES
SMALL_DT = 0
SMALL_F = SSD_HEADS
KAUG_W = 2 * LANES
AUG_TERMS = 3
LOG2E = math.log2(math.e)
Q_SCALE = FOX_HEAD_DIM ** -0.5 * LOG2E
VT_ROWS = 2 * FOX_HEAD_DIM
PAIRS_PER_TRIP = 2


def _softplus(x):
    return jnp.maximum(x, 0.0) + jnp.log1p(jnp.exp(-jnp.abs(x)))


def _silu(x):
    return x * (1.0 / (1.0 + jnp.exp(-x)))


def _lane_cumsum(x):
    n = x.shape[-1]
    lane = lax.broadcasted_iota(jnp.int32, x.shape, x.ndim - 1)
    k = 1
    while k < n:
        x = x + jnp.where(lane >= k, pltpu.roll(x, k, axis=x.ndim - 1), 0.0)
        k *= 2
    return x


def _const_spec(shape):
    zeros = (0,) * len(shape)
    return pl.BlockSpec(shape, lambda *_: zeros, pipeline_mode=pl.Buffered(1))


def _in_proj_kernel(x_ref, g_ref, w_ref, bias_ref, bd_ref, qw_ref, kw_ref, *rest, tiles_per_seq, prompt):
    if prompt:
        (place_ref, z_ref, xbc_ref, q_ref, kf_ref, vf_ref, small_ref, kaug_ref, vt_ref, carry_ref) = rest
    else:
        (z_ref, xbc_ref, q_ref, kf_ref, vf_ref, small_ref, kb_ref, vb_ref) = rest
    x = x_ref[...]
    ms = jnp.mean(x * x, axis=-1, keepdims=True)
    h = (x * lax.rsqrt(ms + EPS) * g_ref[...]).astype(BF16)

    def proj(lo, hi):
        return jnp.dot(h, w_ref[:, lo:hi], preferred_element_type=F32)

    z_ref[...] = proj(C_Z, C_XBC).astype(BF16)
    sm = proj(C_SMALL, N_PROJ) + bias_ref[...]
    lane = lax.broadcasted_iota(jnp.int32, sm.shape, 1)
    sm = jnp.where(lane < SMALL_F, _softplus(sm), -_softplus(-sm))
    small_ref[...] = sm
    if prompt:
        @pl.when(pl.program_id(0) % tiles_per_seq == 0)
        def _():
            carry_ref[...] = jnp.zeros_like(carry_ref)

        cs_t = _lane_cumsum(sm.T) + carry_ref[:, 0:1]
        carry_ref[...] = jnp.broadcast_to(cs_t[:, -1:], carry_ref.shape)
        negc = cs_t.T * (-LOG2E)
        hi = negc.astype(BF16)
        r1 = negc - hi.astype(F32)
        mid = r1.astype(BF16)
        lo = (r1 - mid.astype(F32)).astype(BF16)

    xbc_ref[...] = proj(C_XBC, C_Q).astype(BF16)

    def head_rms(t, w):
        ss = jnp.dot((t * t).astype(BF16), bd_ref[...], preferred_element_type=F32)
        return t * lax.rsqrt(ss * (1.0 / FOX_HEAD_DIM) + EPS) * w

    qn = head_rms(proj(C_Q, C_K), qw_ref[...])
    q_ref[...] = (qn * Q_SCALE).astype(BF16) if prompt else (qn * (FOX_HEAD_DIM ** -0.5)).astype(BF16)
    kn = head_rms(proj(C_K, C_V), kw_ref[...])
    v = proj(C_V, C_SMALL)
    kf_ref[...] = kn.reshape(kf_ref.shape)
    vf_ref[...] = v.reshape(vf_ref.shape)

    kb = kn.astype(BF16)
    if not prompt:
        kb_ref[...] = kb
        vb_ref[...] = v.astype(BF16)
        return

    for p in range(FOX_HEADS // 2):
        kaug_ref[:, p * KAUG_W:p * KAUG_W + LANES] = kb[:, p * LANES:(p + 1) * LANES]
    v_t = v.T
    tm = v_t.shape[1]
    sub = lax.broadcasted_iota(jnp.int32, (VT_ROWS - FOX_HEAD_DIM, tm), 0)
    ones_then_zeros = jnp.where(sub == 0, 1.0, 0.0)
    for hd in range(FOX_HEADS):
        blk = jnp.concatenate([v_t[hd * FOX_HEAD_DIM:(hd + 1) * FOX_HEAD_DIM], ones_then_zeros], axis=0)
        vt_ref[0, hd * VT_ROWS:(hd + 1) * VT_ROWS, :] = blk.astype(BF16)
    aug = (jnp.dot(hi, place_ref[0], preferred_element_type=F32)
           + jnp.dot(mid, place_ref[1], preferred_element_type=F32)
           + jnp.dot(lo, place_ref[2], preferred_element_type=F32)).astype(BF16)
    for p in range(FOX_HEADS // 2):
        kaug_ref[:, p * KAUG_W + LANES:(p + 1) * KAUG_W] = aug[:, p * LANES:(p + 1) * LANES]


def _in_proj(x2d, g, w_all, bias_all, bd, qw, kw, place, *, tm, seq_len, prompt):
    m, d = x2d.shape
    nt = m // tm
    row = lambda i: (i, 0)
    out_shape = [
        jax.ShapeDtypeStruct((m, D_SSM), BF16),
        jax.ShapeDtypeStruct((m, D_CONV), BF16),
        jax.ShapeDtypeStruct((m, D_FOX), BF16),
        jax.ShapeDtypeStruct((m, FOX_HEADS, FOX_HEAD_DIM), F32),
        jax.ShapeDtypeStruct((m, FOX_HEADS, FOX_HEAD_DIM), F32),
        jax.ShapeDtypeStruct((m, LANES), F32),
    ]
    row3 = lambda i: (i, 0, 0)
    out_specs = [
        pl.BlockSpec((tm, D_SSM), row), pl.BlockSpec((tm, D_CONV), row),
        pl.BlockSpec((tm, D_FOX), row),
        pl.BlockSpec((tm, FOX_HEADS, FOX_HEAD_DIM), row3), pl.BlockSpec((tm, FOX_HEADS, FOX_HEAD_DIM), row3),
        pl.BlockSpec((tm, LANES), row),
    ]
    in_specs = [
        pl.BlockSpec((tm, d), row),
        _const_spec((1, d)), _const_spec((d, N_PROJ)), _const_spec((1, LANES)),
        _const_spec((D_FOX, D_FOX)), _const_spec((1, D_FOX)), _const_spec((1, D_FOX)),
    ]
    args = [x2d, g, w_all, bias_all, bd, qw, kw]
    scratch = []
    if prompt:
        in_specs.append(_const_spec(place.shape))
        args.append(place)
        out_shape += [jax.ShapeDtypeStruct((m, (FOX_HEADS // 2) * KAUG_W), BF16),
                      jax.ShapeDtypeStruct((nt, FOX_HEADS * VT_ROWS, tm), BF16)]
        out_specs += [pl.BlockSpec((tm, (FOX_HEADS // 2) * KAUG_W), row),
                      pl.BlockSpec((1, FOX_HEADS * VT_ROWS, tm), lambda i: (i, 0, 0))]
        scratch.append(pltpu.VMEM((LANES, LANES), F32))
    else:
        out_shape += [jax.ShapeDtypeStruct((m, D_FOX), BF16), jax.ShapeDtypeStruct((m, D_FOX), BF16)]
        out_specs += [pl.BlockSpec((tm, D_FOX), row), pl.BlockSpec((tm, D_FOX), row)]
    kern = functools.partial(_in_proj_kernel, tiles_per_seq=max(seq_len // tm, 1), prompt=prompt)
    return pl.pallas_call(
        kern,
        grid=(nt,),
        in_specs=in_specs,
        out_specs=out_specs,
        out_shape=out_shape,
        scratch_shapes=scratch,
        compiler_params=pltpu.CompilerParams(dimension_semantics=("arbitrary",), vmem_limit_bytes=VMEM_LIMIT),
        name="in_proj",
    )(*args)


def _pad_rows(x, rows):
    if x.shape[0] == rows:
        return x
    return jnp.concatenate([x, jnp.zeros((rows - x.shape[0],) + x.shape[1:], x.dtype)], axis=0)


def _ssd_kernel(xbc_ref, small_ref, z_ref, convprev_ref, st0_ref,
                convw_ref, convb_ref, alog_ref, dskip_ref, normw_ref, tri_ref, shift_ref, expand_ref,
                y_ref, stout_ref, convout_ref,
                prev_scr, st_scr, ydiag_scr, *, lc, streams):
    for s in range(streams):
        _ssd_stream(s, xbc_ref, small_ref, z_ref, convprev_ref, st0_ref,
                    convw_ref, convb_ref, alog_ref, dskip_ref, normw_ref, tri_ref, shift_ref, expand_ref,
                    y_ref, stout_ref, convout_ref, prev_scr, st_scr, ydiag_scr, lc)


def _ssd_stream(s, xbc_ref, small_ref, z_ref, convprev_ref, st0_ref,
                convw_ref, convb_ref, alog_ref, dskip_ref, normw_ref, tri_ref, shift_ref, expand_ref,
                y_ref, stout_ref, convout_ref, prev_scr, st_scr, ydiag_scr, lc):
    c = pl.program_id(1)
    last_chunk = c == pl.num_programs(1) - 1
    P = max(lc, LANES)

    @pl.when(c == 0)
    def _():
        prev_scr[s] = convprev_ref[s]
        st_scr[s] = st0_ref[s]

    cur = xbc_ref[s]
    sub = shift_ref.shape[0] // CONV_W
    prev = prev_scr[s]
    convs = []
    for i in range(lc // sub):
        blk = cur[i * sub:(i + 1) * sub]
        shifted = jnp.dot(shift_ref[...], jnp.concatenate([prev, blk], axis=0),
                          preferred_element_type=F32)
        conv = convb_ref[...]
        for k in range(CONV_W):
            conv = conv + convw_ref[k:k + 1, :] * shifted[k * sub:(k + 1) * sub]
        convs.append(conv)
        prev = jnp.concatenate([jnp.zeros((2 * PREV_ROWS, D_CONV), BF16), blk[sub - PREV_ROWS:]], axis=0)
    conv = convs[0] if len(convs) == 1 else jnp.concatenate(convs, axis=0)
    tail = cur[lc - PREV_ROWS:lc]
    prev_scr[s, 0:2 * PREV_ROWS] = jnp.zeros((2 * PREV_ROWS, D_CONV), BF16)
    prev_scr[s, 2 * PREV_ROWS:3 * PREV_ROWS] = tail

    @pl.when(last_chunk)
    def _():
        convout_ref[s] = tail.astype(F32)[PREV_ROWS - SUBLANES:]

    act = _pad_rows(_silu(conv), P)
    dt_c = _pad_rows(small_ref[s], P)
    dt_hi = dt_c.astype(BF16)
    dt_r = dt_c - dt_hi.astype(F32)
    dt_mid = dt_r.astype(BF16)
    dt_lo = (dt_r - dt_mid.astype(F32)).astype(BF16)
    dt = jnp.dot(jnp.concatenate([dt_hi, dt_mid, dt_lo], axis=1), expand_ref[...],
                 preferred_element_type=F32)
    xs = act[:, :D_SSM]
    xdt = xs * dt

    a_all = -jnp.exp(alog_ref[...])
    da = jnp.concatenate([dt, dt_c], axis=1) * a_all
    da_hi = da.astype(BF16)
    da_lo = (da - da_hi.astype(F32)).astype(BF16)
    tri = tri_ref[...]
    acs_all = (jnp.dot(tri, da_hi, preferred_element_type=F32)
               + jnp.dot(tri, da_lo, preferred_element_type=F32))
    acs = acs_all[:, :D_SSM]
    acs_t = acs_all[:, D_SSM:].T
    a_last = acs[P - 1:P, :]
    ea = jnp.exp(acs)
    xt = (xdt * jnp.exp(a_last - acs)).astype(BF16)
    sd = jnp.exp(a_last)

    rowi = lax.broadcasted_iota(jnp.int32, (P, P), 0)
    coli = lax.broadcasted_iota(jnp.int32, (P, P), 1)
    tril = coli <= rowi
    lane = lax.broadcasted_iota(jnp.int32, (P, LANES), 1)
    first_half = lane < SSD_HEAD_DIM

    yoff = []
    for g in range(SSD_GROUPS):
        bg = act[:, D_SSM + g * D_STATE:D_SSM + (g + 1) * D_STATE]
        cg = act[:, D_SSM + D_BC + g * D_STATE:D_SSM + D_BC + (g + 1) * D_STATE]
        bgb = bg.astype(BF16)
        cgb = cg.astype(BF16)
        cb = lax.dot_general(cgb, bgb, (((1,), (1,)), ((), ())), preferred_element_type=F32)
        for pr in range(HEADS_PER_GROUP // 2):
            lo = g * GROUP_W + pr * LANES
            h0 = g * HEADS_PER_GROUP + 2 * pr
            slab = acs[:, lo:lo + LANES]
            rolled = pltpu.roll(slab, SSD_HEAD_DIM, axis=1)
            xpair = xdt[:, lo:lo + LANES]
            yhead = []
            for j in range(2):
                col = jnp.where(first_half, slab, rolled) if j == 0 else jnp.where(first_half, rolled, slab)
                if P > LANES:
                    col = jnp.concatenate([col] * (P // LANES), axis=1)
                seg = col - acs_t[h0 + j:h0 + j + 1, :]
                dec = jnp.exp(jnp.where(tril, seg, -jnp.inf))
                mh = (cb * dec).astype(BF16)
                keep = first_half if j == 0 else jnp.logical_not(first_half)
                xm = jnp.where(keep, xpair, 0.0).astype(BF16)
                yhead.append(jnp.dot(mh, xm, preferred_element_type=F32))
            ydiag_scr[s, :, lo:lo + LANES] = yhead[0] + yhead[1]
        gs = slice(g * GROUP_W, (g + 1) * GROUP_W)
        st = st_scr[s, g]
        yoff.append(jnp.dot(cgb, st.astype(BF16), preferred_element_type=F32) * ea[:, gs])
        st_scr[s, g] = st * sd[:, gs] + jnp.dot(bg.T.astype(BF16), xt[:, gs], preferred_element_type=F32)

    y = ydiag_scr[s] + jnp.concatenate(yoff, axis=1) + xs * dskip_ref[...]
    y = y[:lc] * _silu(z_ref[s].astype(F32))
    outs = []
    for g in range(SSD_GROUPS):
        yg = y[:, g * GROUP_W:(g + 1) * GROUP_W]
        outs.append(yg * lax.rsqrt(jnp.mean(yg * yg, axis=-1, keepdims=True) + EPS))
    y_ref[s] = (jnp.concatenate(outs, axis=1) * normw_ref[...]).astype(BF16)

    @pl.when(last_chunk)
    def _():
        stout_ref[s] = st_scr[s]


def _conv_shift_matrix(lc):
    tap, t = jnp.meshgrid(jnp.arange(CONV_W), jnp.arange(lc), indexing="ij")
    src = (t - (CONV_W - 1) + tap).reshape(-1)
    col = jnp.arange(3 * PREV_ROWS + lc)[None, :]
    in_block = (src[:, None] >= 0) & (col == 3 * PREV_ROWS + src[:, None])
    in_prev = (src[:, None] < 0) & (col < 3 * PREV_ROWS) & (col % PREV_ROWS == PREV_ROWS + src[:, None])
    return (in_block | in_prev).astype(BF16)


def _split_prev_rows(conv_prev):
    b = conv_prev.shape[0]
    p = jnp.concatenate([jnp.zeros((b, PREV_ROWS - (CONV_W - 1), D_CONV), F32), conv_prev.astype(F32)], axis=1)
    hi = p.astype(BF16)
    r1 = p - hi.astype(F32)
    mid = r1.astype(BF16)
    lo = (r1 - mid.astype(F32)).astype(BF16)
    return jnp.concatenate([lo, mid, hi], axis=1)


def _ssd(xbc, small, z, conv_prev, st0, conv_w, conv_b, alog_all, dskip, normw, expand, *, batch, seq_len):
    lc = min(SSD_CHUNK, seq_len)
    assert lc >= PREV_ROWS and seq_len % lc == 0
    nc = seq_len // lc
    p = max(lc, LANES)
    r = jnp.arange(p)
    tri = (r[None, :] <= r[:, None]).astype(BF16)
    shift = _conv_shift_matrix(min(lc, LANES))
    conv_prev = _split_prev_rows(conv_prev)
    ns = SSD_STREAMS if batch % SSD_STREAMS == 0 else 1
    blk = lambda b, c: (b, c, 0)
    per_b3 = lambda b, c: (b, 0, 0)
    per_b4 = lambda b, c: (b, 0, 0, 0)
    per_stream = lambda a: a.reshape(batch, seq_len, a.shape[-1])
    y, st_out, conv_out = pl.pallas_call(
        functools.partial(_ssd_kernel, lc=lc, streams=ns),
        grid=(batch // ns, nc),
        in_specs=[
            pl.BlockSpec((ns, lc, D_CONV), blk),
            pl.BlockSpec((ns, lc, LANES), blk), pl.BlockSpec((ns, lc, D_SSM), blk),
            pl.BlockSpec((ns, 3 * PREV_ROWS, D_CONV), per_b3),
            pl.BlockSpec((ns, SSD_GROUPS, D_STATE, GROUP_W), per_b4),
            _const_spec((CONV_W, D_CONV)), _const_spec((1, D_CONV)), _const_spec((1, D_SSM + LANES)),
            _const_spec((1, D_SSM)), _const_spec((1, D_SSM)), _const_spec((p, p)),
            _const_spec(shift.shape), _const_spec(expand.shape),
        ],
        out_specs=[
            pl.BlockSpec((ns, lc, D_SSM), blk),
            pl.BlockSpec((ns, SSD_GROUPS, D_STATE, GROUP_W), per_b4),
            pl.BlockSpec((ns, SUBLANES, D_CONV), per_b3),
        ],
        out_shape=[
            jax.ShapeDtypeStruct((batch, seq_len, D_SSM), BF16),
            jax.ShapeDtypeStruct((batch, SSD_GROUPS, D_STATE, GROUP_W), F32),
            jax.ShapeDtypeStruct((batch, SUBLANES, D_CONV), F32),
        ],
        scratch_shapes=[
            pltpu.VMEM((ns, 3 * PREV_ROWS, D_CONV), BF16),
            pltpu.VMEM((ns, SSD_GROUPS, D_STATE, GROUP_W), F32),
            pltpu.VMEM((ns, p, D_SSM), F32),
        ],
        compiler_params=pltpu.CompilerParams(dimension_semantics=("arbitrary", "arbitrary"),
                                             vmem_limit_bytes=VMEM_LIMIT),
        name="ssd",
    )(per_stream(xbc), per_stream(small), per_stream(z), conv_prev, st0,
      conv_w, conv_b, alog_all, dskip, normw, tri, shift, expand)
    return y.reshape(batch * seq_len, D_SSM), st_out, conv_out


def _fox_prompt_kernel(q_ref, ka_ref, vt_ref, o_ref, qa_scr, sta_scr, stb_scr, acc_scr, *, t):
    qi = pl.program_id(2)
    lane = lax.broadcasted_iota(jnp.int32, (t, LANES), 1)
    q = q_ref[...]
    for j in range(2):
        own = (lane < FOX_HEAD_DIM) if j == 0 else (lane >= FOX_HEAD_DIM)
        pick = jnp.where(lane >= AUG_TERMS * j, jnp.where(lane < AUG_TERMS * (j + 1), 1.0, 0.0), 0.0)
        qa_scr[j] = jnp.concatenate([jnp.where(own, q, jnp.zeros_like(q)), pick.astype(BF16)], axis=1)
    acc_scr[...] = jnp.zeros(acc_scr.shape, F32)

    def logits(ki, st_scr):
        start = pl.multiple_of(ki * t, t)
        ka = ka_ref[pl.ds(start, t), :]
        for j in range(2):
            st_scr[j] = lax.dot_general(ka, qa_scr[j], (((1,), (1,)), ((), ())),
                                        preferred_element_type=F32)

    def consume(ki, st_scr, m_prev, masked):
        out = []
        for j in range(2):
            st = st_scr[j]
            if masked:
                rowi = lax.broadcasted_iota(jnp.int32, (t, t), 0)
                coli = lax.broadcasted_iota(jnp.int32, (t, t), 1)
                st = jnp.where(rowi <= coli, st, -jnp.inf)
            m_new = jnp.maximum(m_prev[j], jnp.max(st, axis=0, keepdims=True))
            alpha = jnp.exp2(m_prev[j] - m_new)
            pexp = jnp.exp2(st - m_new).astype(BF16)
            vt = vt_ref[ki, j * VT_ROWS:(j + 1) * VT_ROWS, :]
            acc_scr[j] = alpha * acc_scr[j] + jnp.dot(vt, pexp, preferred_element_type=F32)
            out.append(m_new)
        return tuple(out)

    logits(0, sta_scr)

    def pair(k0, m):
        logits(k0 + 1, stb_scr)
        m = consume(k0, sta_scr, m, False)
        logits(k0 + 2, sta_scr)
        return consume(k0 + 1, stb_scr, m, False)

    def pairs(k0, m):
        for u in range(PAIRS_PER_TRIP):
            m = pair(k0 + 2 * u, m)
        return m

    span = 2 * PAIRS_PER_TRIP
    trips = qi // span
    m0 = tuple(jnp.full((1, t), -jnp.inf, F32) for _ in range(2))
    m = lax.fori_loop(0, trips, lambda i, m: pairs(span * i, m), m0)
    m = lax.fori_loop(0, (qi - span * trips) // 2, lambda i, m: pair(span * trips + 2 * i, m), m)
    odd = lax.rem(qi, 2) == 1

    @pl.when(jnp.logical_not(odd))
    def _():
        consume(qi, sta_scr, m, True)

    @pl.when(odd)
    def _():
        logits(qi, stb_scr)
        m1 = consume(qi - 1, sta_scr, m, False)
        consume(qi, stb_scr, m1, True)

    halves = []
    for j in range(2):
        acc = acc_scr[j]
        halves.append(acc[:FOX_HEAD_DIM] * (1.0 / acc[FOX_HEAD_DIM:FOX_HEAD_DIM + 1]))
    o_ref[...] = jnp.concatenate(halves, axis=0).T.astype(BF16)


def _fox_prompt(q, kaug, vt, *, batch, seq_len, t):
    nq = seq_len // t
    m = batch * seq_len
    return pl.pallas_call(
        functools.partial(_fox_prompt_kernel, t=t),
        grid=(batch, FOX_HEADS // 2, nq),
        in_specs=[
            pl.BlockSpec((t, LANES), lambda b, p, i: (b * nq + i, p)),
            pl.BlockSpec((seq_len, KAUG_W), lambda b, p, i: (b, p)),
            pl.BlockSpec((nq, 2 * VT_ROWS, t), lambda b, p, i: (b, p, 0)),
        ],
        out_specs=pl.BlockSpec((t, LANES), lambda b, p, i: (b * nq + i, p)),
        out_shape=jax.ShapeDtypeStruct((m, D_FOX), BF16),
        scratch_shapes=[
            pltpu.VMEM((2, t, KAUG_W), BF16),
            pltpu.VMEM((2, t, t), F32),
            pltpu.VMEM((2, t, t), F32),
            pltpu.VMEM((2, VT_ROWS, t), F32),
        ],
        compiler_params=pltpu.CompilerParams(dimension_semantics=("arbitrary", "arbitrary", "arbitrary"),
                                             vmem_limit_bytes=VMEM_LIMIT),
        name="fox_prompt",
    )(q, kaug, vt)


def _fox_sample_kernel(q_ref, kc_ref, vc_ref, lfc_ref, kn_ref, vn_ref, lfn_ref, own_ref, qpos_ref, o_ref,
                       qbd_scr, m_scr, l_scr, acc_scr, carry_scr, *, tn):
    j = pl.program_id(1)
    nk = pl.num_programs(1)
    rows = FOX_HEADS * tn
    own = own_ref[...] > 0.0

    @pl.when(j == 0)
    def _():
        qt = jnp.concatenate([q_ref[...]] * FOX_HEADS, axis=0)
        qbd_scr[...] = jnp.where(own, qt, jnp.zeros_like(qt))
        m_scr[...] = jnp.full(m_scr.shape, -jnp.inf, F32)
        l_scr[...] = jnp.zeros(l_scr.shape, F32)
        acc_scr[...] = jnp.zeros(acc_scr.shape, F32)
        carry_scr[...] = jnp.zeros(carry_scr.shape, F32)

    def expand(ct):
        n = ct.shape[-1]
        return jnp.concatenate([jnp.broadcast_to(ct[h:h + 1], (tn, n)) for h in range(FOX_HEADS)], axis=0)

    nt_dims = (((1,), (1,)), ((), ()))

    def update(s, vals, vals_transposed):
        m_prev = m_scr[...]
        m_new = jnp.maximum(m_prev, jnp.max(s, axis=-1, keepdims=True))
        alpha = jnp.exp(m_prev - m_new)
        pexp = jnp.exp(s - m_new)
        l_scr[...] = alpha * l_scr[...] + jnp.sum(pexp, axis=-1, keepdims=True)
        if vals_transposed:
            pv = lax.dot_general(pexp.astype(BF16), vals, nt_dims, preferred_element_type=F32)
        else:
            pv = jnp.dot(pexp.astype(BF16), vals, preferred_element_type=F32)
        acc_scr[...] = alpha * acc_scr[...] + pv
        m_scr[...] = m_new

    ct = _lane_cumsum(lfc_ref[0]) + carry_scr[:, 0:1]
    carry_scr[...] = jnp.broadcast_to(ct[:, -1:], carry_scr.shape)
    tk = kc_ref.shape[-1]
    kt = kc_ref[0].reshape(D_FOX, tk).astype(BF16)
    s = jnp.dot(qbd_scr[...], kt, preferred_element_type=F32) - expand(ct)
    update(s, vc_ref[0].reshape(D_FOX, tk).astype(BF16), True)

    @pl.when(j == nk - 1)
    def _():
        cn = _lane_cumsum(lfn_ref[0]) + carry_scr[:, 0:1]
        kn = _pad_rows(kn_ref[...], LANES)
        vn = _pad_rows(vn_ref[...], LANES)
        sn = lax.dot_general(qbd_scr[...], kn, nt_dims, preferred_element_type=F32) - expand(cn)
        cidx = lax.broadcasted_iota(jnp.int32, (rows, LANES), 1)
        sn = jnp.where(cidx <= qpos_ref[...], sn, -jnp.inf)
        update(sn, vn, False)
        o = acc_scr[...] / l_scr[...]
        o = jnp.where(own, o, 0.0)
        out = o[0:tn]
        for h in range(1, FOX_HEADS):
            out = out + o[h * tn:(h + 1) * tn]
        o_ref[...] = out.astype(BF16)


def _fox_sample(q, kc, vc, lfc_t, kn, vn, lfn_t, *, batch, tn, past, tk):
    nk = past // tk
    rows = FOX_HEADS * tn
    r = jnp.arange(rows)
    own = ((r // tn)[:, None] == (jnp.arange(D_FOX) // FOX_HEAD_DIM)[None, :]).astype(F32)
    qpos = (r % tn).astype(jnp.int32).reshape(rows, 1)
    return pl.pallas_call(
        functools.partial(_fox_sample_kernel, tn=tn),
        grid=(batch, nk),
        in_specs=[
            pl.BlockSpec((tn, D_FOX), lambda b, j: (b, 0)),
            pl.BlockSpec((1, FOX_HEADS, FOX_HEAD_DIM, tk), lambda b, j: (b, 0, 0, j)),
            pl.BlockSpec((1, FOX_HEADS, FOX_HEAD_DIM, tk), lambda b, j: (b, 0, 0, j)),
            pl.BlockSpec((1, FOX_HEADS, tk), lambda b, j: (b, 0, j)),
            pl.BlockSpec((tn, D_FOX), lambda b, j: (b, 0)),
            pl.BlockSpec((tn, D_FOX), lambda b, j: (b, 0)),
            pl.BlockSpec((1, FOX_HEADS, LANES), lambda b, j: (b, 0, 0)),
            _const_spec((rows, D_FOX)), _const_spec((rows, 1)),
        ],
        out_specs=pl.BlockSpec((tn, D_FOX), lambda b, j: (b, 0)),
        out_shape=jax.ShapeDtypeStruct((batch * tn, D_FOX), BF16),
        scratch_shapes=[
            pltpu.VMEM((rows, D_FOX), BF16),
            pltpu.VMEM((rows, 1), F32),
            pltpu.VMEM((rows, 1), F32),
            pltpu.VMEM((rows, D_FOX), F32),
            pltpu.VMEM((FOX_HEADS, LANES), F32),
        ],
        compiler_params=pltpu.CompilerParams(dimension_semantics=("arbitrary", "arbitrary"),
                                             vmem_limit_bytes=VMEM_LIMIT),
        name="fox_sample",
    )(q, kc, vc, lfc_t, kn, vn, lfn_t, own, qpos)


def _out_mlp_kernel(x_ref, ys_ref, yf_ref, wos_ref, wof_ref, g_ref, wup_ref, wdn_ref, o_ref):
    x1 = (x_ref[...]
          + jnp.dot(ys_ref[...], wos_ref[...], preferred_element_type=F32)
          + jnp.dot(yf_ref[...], wof_ref[...], preferred_element_type=F32))
    ms = jnp.mean(x1 * x1, axis=-1, keepdims=True)
    h = (x1 * lax.rsqrt(ms + EPS) * g_ref[...]).astype(BF16)
    up = jnp.dot(h, wup_ref[...], preferred_element_type=F32)
    a = jnp.square(jnp.maximum(up, 0.0)).astype(BF16)
    o_ref[...] = x1 + jnp.dot(a, wdn_ref[...], preferred_element_type=F32)


def _out_mlp(x2d, ys, yf, wo_s, wo_f, g2, w_up, w_dn, *, tm):
    m, d = x2d.shape
    d_ff = w_up.shape[1]
    row = lambda i: (i, 0)
    return pl.pallas_call(
        _out_mlp_kernel,
        grid=(m // tm,),
        in_specs=[
            pl.BlockSpec((tm, d), row), pl.BlockSpec((tm, D_SSM), row), pl.BlockSpec((tm, D_FOX), row),
            _const_spec((D_SSM, d)), _const_spec((D_FOX, d)), _const_spec((1, d)),
            _const_spec((d, d_ff)), _const_spec((d_ff, d)),
        ],
        out_specs=pl.BlockSpec((tm, d), row),
        out_shape=jax.ShapeDtypeStruct((m, d), F32),
        compiler_params=pltpu.CompilerParams(dimension_semantics=("arbitrary",), vmem_limit_bytes=VMEM_LIMIT),
        name="out_mlp",
    )(x2d, ys, yf, wo_s, wo_f, g2, w_up, w_dn)


def _state_to_kernel_layout(st):
    b = st.shape[0]
    st = st.reshape(b, SSD_GROUPS, HEADS_PER_GROUP, SSD_HEAD_DIM, D_STATE)
    return st.transpose(0, 1, 4, 2, 3).reshape(b, SSD_GROUPS, D_STATE, GROUP_W)


def _state_from_kernel_layout(st):
    b = st.shape[0]
    st = st.reshape(b, SSD_GROUPS, D_STATE, HEADS_PER_GROUP, SSD_HEAD_DIM)
    return st.transpose(0, 1, 3, 4, 2).reshape(b, SSD_HEADS, SSD_HEAD_DIM, D_STATE)


def _layer_weights(norm1_w, w_in, conv_w, conv_b, dt_bias, A_log, D_skip, ssd_norm_w, f_bias,
                   q_norm_w, k_norm_w, w_out, norm2_w, w_up, w_down):
    d = w_in.shape[0]
    splits = [D_SSM, D_CONV, SSD_HEADS, D_FOX, D_FOX, D_FOX]
    idx = [sum(splits[:i + 1]) for i in range(len(splits))]
    w_z, w_xbc, w_dt, w_q, w_k, w_v, w_f = jnp.split(w_in, idx, axis=1)
    pad = jnp.zeros((d, LANES - SSD_HEADS - FOX_HEADS), w_in.dtype)
    w_all = jnp.concatenate(
        [w_z, w_xbc, w_q, w_k, w_v, w_dt, w_f, pad], axis=1).astype(BF16)
    bias_all = jnp.concatenate(
        [dt_bias, f_bias, jnp.zeros((LANES - SSD_HEADS - FOX_HEADS,), F32)]).reshape(1, LANES).astype(F32)
    src = jnp.arange(AUG_TERMS * LANES) % LANES
    expand = (src[:, None] == (jnp.arange(D_SSM) // SSD_HEAD_DIM)[None, :]).astype(BF16)
    head_id = jnp.arange(D_FOX) // FOX_HEAD_DIM
    bd = (head_id[:, None] == head_id[None, :]).astype(BF16)
    alog_all = jnp.concatenate(
        [jnp.repeat(A_log, SSD_HEAD_DIM), A_log, jnp.zeros((LANES - SSD_HEADS,), F32)]).reshape(1, D_SSM + LANES)
    hh = jnp.arange(FOX_HEADS)
    place = jnp.zeros((AUG_TERMS, LANES, D_FOX), F32)
    for t in range(AUG_TERMS):
        place = place.at[t, SMALL_F + hh, (hh // 2) * LANES + (hh % 2) * AUG_TERMS + t].set(1.0)
    place = place.astype(BF16)
    return dict(
        g1=norm1_w.reshape(1, d).astype(F32), w_all=w_all, bias_all=bias_all, bd=bd, expand=expand,
        qw=jnp.tile(q_norm_w, FOX_HEADS).reshape(1, D_FOX).astype(F32),
        kw=jnp.tile(k_norm_w, FOX_HEADS).reshape(1, D_FOX).astype(F32),
        conv_w=conv_w.astype(F32), conv_b=conv_b.reshape(1, D_CONV).astype(F32), alog_all=alog_all.astype(F32),
        dskip=jnp.repeat(D_skip, SSD_HEAD_DIM).reshape(1, D_SSM).astype(F32),
        normw=ssd_norm_w.reshape(1, D_SSM).astype(F32), place=place,
        wo_s=w_out[:D_SSM].astype(BF16), wo_f=w_out[D_SSM:].astype(BF16),
        g2=norm2_w.reshape(1, d).astype(F32), w_up=w_up.astype(BF16), w_dn=w_down.astype(BF16),
    )


def _pick_tile(n, pref):
    t = min(pref, n)
    while n % t:
        t //= 2
    return t


def _trunk(x, conv_prev, ssm_prev, w, fox_fn, *, prompt):
    b, l, d = x.shape
    m = b * l
    x2d = x.reshape(m, d)
    tm = _pick_tile(l if prompt else m, 512)
    z, xbc, q, kf, vf, small, ka, va = _in_proj(
        x2d, w["g1"], w["w_all"], w["bias_all"], w["bd"], w["qw"], w["kw"], w["place"],
        tm=tm, seq_len=l, prompt=prompt)
    y_ssd, st_out, conv_out = _ssd(xbc, small, z, conv_prev, _state_to_kernel_layout(ssm_prev.astype(F32)),
                                   w["conv_w"], w["conv_b"], w["alog_all"], w["dskip"], w["normw"],
                                   w["expand"], batch=b, seq_len=l)
    y_fox = fox_fn(q, ka, va, small, tm)
    y = _out_mlp(x2d, y_ssd, y_fox, w["wo_s"], w["wo_f"], w["g2"], w["w_up"], w["w_dn"], tm=_pick_tile(m, 512))
    logf = small[:, SMALL_F:SMALL_F + FOX_HEADS].reshape(b, l, FOX_HEADS)
    states = (kf.reshape(b, l, FOX_HEADS, FOX_HEAD_DIM), vf.reshape(b, l, FOX_HEADS, FOX_HEAD_DIM), logf,
              _state_from_kernel_layout(st_out), conv_out[:, SUBLANES - (CONV_W - 1):, :])
    return y.reshape(b, l, d), states


def kernel(x_prompt, x_sample, cache_k, cache_v, cache_logf, state_ssm, state_conv, norm1_w, w_in, conv_w, conv_b, dt_bias, A_log, D_skip, ssd_norm_w, f_bias, q_norm_w, k_norm_w, w_out, norm2_w, w_up, w_down):
    depth = w_in.shape[0]
    b_p, l_p, _ = x_prompt.shape
    b_s, l_s, _ = x_sample.shape
    past = cache_k.shape[2]
    y_prompt, y_sample = x_prompt, x_sample
    p_states, s_states = [], []
    for layer in range(depth):
        w = _layer_weights(norm1_w[layer], w_in[layer], conv_w[layer], conv_b[layer], dt_bias[layer], A_log[layer],
                           D_skip[layer], ssd_norm_w[layer], f_bias[layer], q_norm_w[layer], k_norm_w[layer],
                           w_out[layer], norm2_w[layer], w_up[layer], w_down[layer])

        def fox_prompt_fn(q, kaug, vt, small, tm):
            return _fox_prompt(q, kaug, vt, batch=b_p, seq_len=l_p, t=tm)

        def fox_sample_fn(q, kb, vb, small, tm, layer=layer):
            lf_new = small[:, SMALL_F:SMALL_F + FOX_HEADS].reshape(b_s, l_s, FOX_HEADS).transpose(0, 2, 1)
            lf_new = jnp.pad(lf_new, ((0, 0), (0, 0), (0, LANES - l_s)))
            lf_cache = cache_logf[layer].astype(F32).transpose(0, 2, 1)
            return _fox_sample(q, cache_k[layer].transpose(0, 2, 3, 1), cache_v[layer].transpose(0, 2, 3, 1),
                               lf_cache, kb, vb, lf_new, batch=b_s, tn=l_s, past=past, tk=_pick_tile(past, 4096))

        conv0 = jnp.zeros((b_p, CONV_W - 1, D_CONV), F32)
        ssm0 = jnp.zeros((b_p, SSD_HEADS, SSD_HEAD_DIM, D_STATE), F32)
        y_prompt, st_p = _trunk(y_prompt, conv0, ssm0, w, fox_prompt_fn, prompt=True)
        y_sample, st_s = _trunk(y_sample, state_conv[layer], state_ssm[layer], w, fox_sample_fn, prompt=False)
        p_states.append(st_p)
        s_states.append(st_s)
    stack = lambda states, i: jnp.stack([s[i] for s in states])
    return (y_prompt, y_sample,
            stack(p_states, 0), stack(p_states, 1), stack(p_states, 2), stack(p_states, 3), stack(p_states, 4),
            stack(s_states, 0), stack(s_states, 1), stack(s_states, 2), stack(s_states, 3), stack(s_states, 4))
```
